```python
import jax, jax.numpy as jnp
from jax import lax
import numpy as np

D_MODEL = 1024
BATCH = 2
SEQ = 8192
DEPTH = 4

CHUNK = 128
A_HEADS = 4
A_WIDTH = D_MODEL // 2
A_HEAD_DIM = A_WIDTH // A_HEADS
B_WIDTH = D_MODEL // 2
B_GROUPS = 4
CONV_B_WIDTH = 31
CONV_B_PAD = CONV_B_WIDTH // 2
EVEN_IN = 2 * A_WIDTH + 2 * B_WIDTH
EVEN_MIX = A_WIDTH + B_WIDTH
C_WIDTH = D_MODEL
CONV_C_WIDTH = 3
CONV_C_PAD = CONV_C_WIDTH // 2
D_FF = 2816
N_EXPERTS = 8
TOP_K = 2
RMS_EPS = 1e-6
LN_EPS = 1e-5

kernel_name = "hybrid_gmlp_conformer_shortconv_moe_encoder"


def rmsnorm(x, g):
    xf = x.astype(jnp.float32)
    y = xf * lax.rsqrt(jnp.mean(xf * xf, axis=-1, keepdims=True) + RMS_EPS)
    return (y * g.astype(jnp.float32)).astype(x.dtype)


def layernorm(x, g, b):
    xf = x.astype(jnp.float32)
    mu = jnp.mean(xf, axis=-1, keepdims=True)
    xc = xf - mu
    y = xc * lax.rsqrt(jnp.mean(xc * xc, axis=-1, keepdims=True) + LN_EPS)
    return (y * g.astype(jnp.float32) + b.astype(jnp.float32)).astype(x.dtype)


def depthwise_conv(x, w, pad):
    c = x.shape[-1]
    return lax.conv_general_dilated(
        x, w[:, None, :].astype(x.dtype), window_strides=(1,), padding=[(pad, pad)],
        dimension_numbers=("NWC", "WIO", "NWC"), feature_group_count=c)


def swiglu(x, w1, w3, w2):
    return (jax.nn.silu(x @ w1) * (x @ w3)) @ w2


def mixer_ab(h, w_in, ws, bs, ln_a_g, ln_a_b, conv_w, conv_b, ln_b_g, ln_b_b, w_out):
    bsz, s, _ = h.shape
    z = h @ w_in
    a_z, b_a, b_g = z[..., :2 * A_WIDTH], z[..., 2 * A_WIDTH:2 * A_WIDTH + B_WIDTH], z[..., 2 * A_WIDTH + B_WIDTH:]
    a_z = jax.nn.gelu(a_z)
    a_u, a_v = a_z[..., :A_WIDTH], a_z[..., A_WIDTH:]
    a_v = layernorm(a_v, ln_a_g, ln_a_b)
    v = a_v.reshape(bsz, s // CHUNK, CHUNK, A_HEADS, A_HEAD_DIM)
    v = jnp.einsum("hts,bcshd->bcthd", ws, v) + bs.T[None, None, :, :, None]
    a_out = a_u * v.reshape(bsz, s, A_WIDTH)
    glu = b_a * jax.nn.sigmoid(b_g)
    c = depthwise_conv(glu, conv_w, CONV_B_PAD) + conv_b
    b_out = jax.nn.silu(layernorm(c, ln_b_g, ln_b_b))
    return jnp.concatenate([a_out, b_out], axis=-1) @ w_out


def mixer_c(h, w_in, conv_w, w_out):
    z = h @ w_in
    gate_b, gate_c, v = z[..., :C_WIDTH], z[..., C_WIDTH:2 * C_WIDTH], z[..., 2 * C_WIDTH:]
    y = gate_b * depthwise_conv(gate_c * v, conv_w, CONV_C_PAD)
    return y @ w_out


def moe_swiglu(h, router, w1, w3, w2):
    bsz, s, d = h.shape
    xf = h.reshape(bsz * s, d)
    logits = (xf @ router).astype(jnp.float32)
    top_v, top_i = lax.top_k(logits, TOP_K)
    gate = jax.nn.softmax(top_v, axis=-1)
    combine = jnp.einsum("nk,nke->ne", gate,
                         jax.nn.one_hot(top_i, N_EXPERTS, dtype=jnp.float32)).astype(h.dtype)
    out = jnp.zeros_like(xf)
    for e in range(N_EXPERTS):
        out = out + combine[:, e:e + 1] * swiglu(xf, w1[e], w3[e], w2[e])
    return out.reshape(bsz, s, d)


def setup_inputs(seed: int = 0) -> dict:
    key = jax.random.key(seed)
    ks = iter(jax.random.split(key, 32))
    n_even = (DEPTH + 1) // 2
    n_odd = DEPTH // 2
    f32 = jnp.float32

    def nrm(shape, scale):
        return jax.random.normal(next(ks), shape, f32) * scale

    def gain(shape):
        return 1.0 + 0.02 * jax.random.normal(next(ks), shape, f32)

    D = D_MODEL
    return {
        "x": jax.random.normal(next(ks), (BATCH, SEQ, D), f32),
        "norm_mix_g": gain((DEPTH, D)),
        "norm_ffn_g": gain((DEPTH, D)),
        "ev_w_in": nrm((n_even, D, EVEN_IN), D ** -0.5),
        "ev_ws": nrm((n_even, A_HEADS, CHUNK, CHUNK), CHUNK ** -0.5),
        "ev_bs": gain((n_even, A_HEADS, CHUNK)),
        "ev_ln_a_g": gain((n_even, A_WIDTH)),
        "ev_ln_a_b": nrm((n_even, A_WIDTH), 0.02),
        "ev_conv_w": nrm((n_even, CONV_B_WIDTH, B_WIDTH), CONV_B_WIDTH ** -0.5),
        "ev_conv_b": nrm((n_even, B_WIDTH), 0.02),
        "ev_ln_b_g": gain((n_even, B_WIDTH)),
        "ev_ln_b_b": nrm((n_even, B_WIDTH), 0.02),
        "ev_w_out": nrm((n_even, EVEN_MIX, D), EVEN_MIX ** -0.5),
        "od_w_in": nrm((n_odd, D, 3 * C_WIDTH), D ** -0.5),
        "od_conv_w": nrm((n_odd, CONV_C_WIDTH, C_WIDTH), CONV_C_WIDTH ** -0.5),
        "od_w_out": nrm((n_odd, C_WIDTH, D), C_WIDTH ** -0.5),
        "ffn_w1": nrm((n_even, D, D_FF), D ** -0.5),
        "ffn_w3": nrm((n_even, D, D_FF), D ** -0.5),
        "ffn_w2": nrm((n_even, D_FF, D), D_FF ** -0.5),
        "moe_router": nrm((n_odd, D, N_EXPERTS), D ** -0.5),
        "moe_w1": nrm((n_odd, N_EXPERTS, D, D_FF), D ** -0.5),
        "moe_w3": nrm((n_odd, N_EXPERTS, D, D_FF), D ** -0.5),
        "moe_w2": nrm((n_odd, N_EXPERTS, D_FF, D), D_FF ** -0.5),
        "final_g": gain((D,)),
    }


def reference(x, norm_mix_g, norm_ffn_g, ev_w_in, ev_ws, ev_bs, ev_ln_a_g, ev_ln_a_b,
              ev_conv_w, ev_conv_b, ev_ln_b_g, ev_ln_b_b, ev_w_out,
              od_w_in, od_conv_w, od_w_out, ffn_w1, ffn_w3, ffn_w2,
              moe_router, moe_w1, moe_w3, moe_w2, final_g):
    for layer in range(DEPTH):
        i = layer // 2
        hn = rmsnorm(x, norm_mix_g[layer])
        if layer % 2 == 0:
            x = x + mixer_ab(hn, ev_w_in[i], ev_ws[i], ev_bs[i], ev_ln_a_g[i], ev_ln_a_b[i],
                             ev_conv_w[i], ev_conv_b[i], ev_ln_b_g[i], ev_ln_b_b[i], ev_w_out[i])
            hn = rmsnorm(x, norm_ffn_g[layer])
            x = x + swiglu(hn, ffn_w1[i], ffn_w3[i], ffn_w2[i])
        else:
            x = x + mixer_c(hn, od_w_in[i], od_conv_w[i], od_w_out[i])
            hn = rmsnorm(x, norm_ffn_g[layer])
            x = x + moe_swiglu(hn, moe_router[i], moe_w1[i], moe_w3[i], moe_w2[i])
    return rmsnorm(x, final_g)
```

```python
import functools

import jax
import jax.numpy as jnp
from jax import lax
from jax.experimental import pallas as pl
from jax.experimental.pallas import tpu as pltpu

D_MODEL = 1024
A_WIDTH = 512
A_HEADS = 4
CHUNK = 128
B_WIDTH = 512
CONV_B_WIDTH = 31
CONV_B_PAD = 15
CONV_C_WIDTH = 3
D_FF = 2816
N_EXPERTS = 8
TOP_K = 2
RMS_EPS = 1e-6
LN_EPS = 1e-5

LANES = 128
TM = 512
HALO_B = 16
HALO_C = 8
FF_CHUNK = 256
N_FF_CHUNKS = D_FF // FF_CHUNK
CONV_ROWS = 64
VMEM_LIMIT = 60 * 1024 * 1024

_F32 = jnp.float32
_BF16 = jnp.bfloat16


def _rms(x, g):
    y = x * lax.rsqrt(jnp.mean(x * x, axis=-1, keepdims=True) + RMS_EPS)
    return y * g


def _layernorm(x, g, b):
    mu = jnp.mean(x, axis=-1, keepdims=True)
    xc = x - mu
    y = xc * lax.rsqrt(jnp.mean(xc * xc, axis=-1, keepdims=True) + LN_EPS)
    return y * g + b


def _sigmoid(x):
    return 1.0 / (1.0 + jnp.exp(-x))


def _gelu_tanh(x):
    c = 0.7978845608028654
    return 0.5 * x * (1.0 + jnp.tanh(c * (x + 0.044715 * (x * x * x))))


def _dot(a, b):
    return jnp.dot(a, b, preferred_element_type=_F32)


def _sum_refs(refs):
    x = refs[0][...]
    for r in refs[1:]:
        x = x + r[...]
    return x


def _mixer_ab_kernel(n_add, tiles_per_seq, *refs):
    mains = refs[0:n_add]
    prevs = refs[n_add:2 * n_add]
    nexts = refs[2 * n_add:3 * n_add]
    (g_ref, w_in_ref, ws_ref, bsb_ref, lnag_ref, lnab_ref, cw_ref, cb_ref,
     lnbg_ref, lnbb_ref, w_out_ref, o_ref, glu_scr, mix_scr) = refs[3 * n_add:]

    i = pl.program_id(0)
    first = (i % tiles_per_seq) == 0
    last = (i % tiles_per_seq) == tiles_per_seq - 1

    x = _sum_refs(mains)
    g = g_ref[...]
    xn = _rms(x, g).astype(_BF16)
    z = _dot(xn, w_in_ref[...])

    xh = jnp.concatenate([_sum_refs(prevs), _sum_refs(nexts)], axis=0)
    xhn = _rms(xh, g).astype(_BF16)
    zh = _dot(xhn, w_in_ref[:, 2 * A_WIDTH:])
    glu_h = zh[:, :B_WIDTH] * _sigmoid(zh[:, B_WIDTH:])
    glu_scr[0:HALO_B, :] = jnp.where(first, 0.0, glu_h[:HALO_B])
    glu_scr[HALO_B + TM:, :] = jnp.where(last, 0.0, glu_h[HALO_B:])
    glu_scr[HALO_B:HALO_B + TM, :] = (
        z[:, 2 * A_WIDTH:2 * A_WIDTH + B_WIDTH] * _sigmoid(z[:, 2 * A_WIDTH + B_WIDTH:]))

    a_u = _gelu_tanh(z[:, :A_WIDTH])
    a_v = _layernorm(_gelu_tanh(z[:, A_WIDTH:2 * A_WIDTH]), lnag_ref[...], lnab_ref[...])
    a_vb = a_v.astype(_BF16)
    for c in range(TM // CHUNK):
        rows = slice(c * CHUNK, (c + 1) * CHUNK)
        for h in range(A_HEADS):
            cols = slice(h * LANES, (h + 1) * LANES)
            v = _dot(ws_ref[h], a_vb[rows, cols]) + bsb_ref[h]
            mix_scr[rows, cols] = (a_u[rows, cols] * v).astype(_BF16)

    cb = cb_ref[...]
    for r in range(TM // CONV_ROWS):
        base = r * CONV_ROWS
        acc = jnp.zeros((CONV_ROWS, B_WIDTH), _F32)
        for k in range(CONV_B_WIDTH):
            off = base + HALO_B - CONV_B_PAD + k
            acc = acc + glu_scr[off:off + CONV_ROWS, :] * cw_ref[k:k + 1, :]
        y = _layernorm(acc + cb, lnbg_ref[...], lnbb_ref[...])
        mix_scr[base:base + CONV_ROWS, A_WIDTH:] = (y * _sigmoid(y)).astype(_BF16)

    o_ref[...] = x + _dot(mix_scr[...], w_out_ref[...])


def _const_spec(shape):
    nd = len(shape)
    return pl.BlockSpec(shape, lambda *_: (0,) * nd, pipeline_mode=pl.Buffered(1))


def _row_specs(n_add, n_rows, halo, row_offsets):
    nb = n_rows // halo
    per = TM // halo
    mains, prevs, nexts = [], [], []
    for off in row_offsets:
        mains.append(pl.BlockSpec((TM, D_MODEL), lambda i, o=off: (i + o // TM, 0)))
        prevs.append(pl.BlockSpec(
            (halo, D_MODEL), lambda i, o=off: (jnp.maximum(i * per - 1, 0) + o // halo, 0)))
        nexts.append(pl.BlockSpec(
            (halo, D_MODEL), lambda i, o=off: (jnp.minimum((i + 1) * per, nb - 1) + o // halo, 0)))
    return mains + prevs + nexts


def _mixer_ab(addends, row_offsets, n_rows, seq, p):
    n_add = len(addends)
    in_specs = _row_specs(n_add, n_rows, HALO_B, row_offsets) + [
        _const_spec((1, D_MODEL)),
        _const_spec((D_MODEL, 2 * A_WIDTH + 2 * B_WIDTH)),
        _const_spec((A_HEADS, CHUNK, CHUNK)),
        _const_spec((A_HEADS, CHUNK, LANES)),
        _const_spec((1, A_WIDTH)), _const_spec((1, A_WIDTH)),
        _const_spec((CONV_B_WIDTH, B_WIDTH)), _const_spec((1, B_WIDTH)),
        _const_spec((1, B_WIDTH)), _const_spec((1, B_WIDTH)),
        _const_spec((A_WIDTH + B_WIDTH, D_MODEL)),
    ]
    return pl.pallas_call(
        functools.partial(_mixer_ab_kernel, n_add, seq // TM),
        grid=(n_rows // TM,),
        in_specs=in_specs,
        out_specs=pl.BlockSpec((TM, D_MODEL), lambda i: (i, 0)),
        out_shape=jax.ShapeDtypeStruct((n_rows, D_MODEL), _F32),
        scratch_shapes=[pltpu.VMEM((TM + 2 * HALO_B, B_WIDTH), _F32),
                        pltpu.VMEM((TM, A_WIDTH + B_WIDTH), _BF16)],
        compiler_params=pltpu.CompilerParams(
            dimension_semantics=("arbitrary",), vmem_limit_bytes=VMEM_LIMIT),
        name="mixer_ab",
    )(*(addends * 3), p["g"], p["w_in"], p["ws"], p["bsb"], p["ln_a_g"], p["ln_a_b"],
      p["conv_w"], p["conv_b"], p["ln_b_g"], p["ln_b_b"], p["w_out"])


def _swiglu_into(acc_ref, xn, w13_ref, w2_ref):
    acc_ref[...] = jnp.zeros_like(acc_ref)

    def body(c, carry):
        h = _dot(xn, w13_ref[c])
        h1 = h[:, :FF_CHUNK]
        a = (h1 * _sigmoid(h1) * h[:, FF_CHUNK:]).astype(_BF16)
        acc_ref[...] += _dot(a, w2_ref[c])
        return carry

    lax.fori_loop(0, N_FF_CHUNKS, body, 0)


def _ffn_kernel(x_ref, g_ref, w13_ref, w2_ref, o_ref, acc_ref):
    x = x_ref[...]
    xn = _rms(x, g_ref[...]).astype(_BF16)
    _swiglu_into(acc_ref, xn, w13_ref, w2_ref)
    o_ref[...] = x + acc_ref[...]


def _ffn(x, g, w13, w2):
    n_rows = x.shape[0]
    return pl.pallas_call(
        _ffn_kernel,
        grid=(n_rows // TM,),
        in_specs=[pl.BlockSpec((TM, D_MODEL), lambda i: (i, 0)),
                  _const_spec((1, D_MODEL)),
                  _const_spec((N_FF_CHUNKS, D_MODEL, 2 * FF_CHUNK)),
                  _const_spec((N_FF_CHUNKS, FF_CHUNK, D_MODEL))],
        out_specs=pl.BlockSpec((TM, D_MODEL), lambda i: (i, 0)),
        out_shape=jax.ShapeDtypeStruct((n_rows, D_MODEL), _F32),
        scratch_shapes=[pltpu.VMEM((TM, D_MODEL), _F32)],
        compiler_params=pltpu.CompilerParams(
            dimension_semantics=("arbitrary",), vmem_limit_bytes=VMEM_LIMIT),
        name="ffn",
    )(x, g, w13, w2)


def _mixer_c_kernel(tiles_per_seq, x_ref, xp_ref, xq_ref, g_ref, w_in_ref, cw_ref, w_out_ref,
                    g2_ref, rhi_ref, rlo_ref, o_ref, route_ref, u_scr):
    i = pl.program_id(0)
    first = (i % tiles_per_seq) == 0
    last = (i % tiles_per_seq) == tiles_per_seq - 1

    x = x_ref[...]
    g = g_ref[...]
    xn = _rms(x, g).astype(_BF16)
    z = _dot(xn, w_in_ref[...])

    xh = jnp.concatenate([xp_ref[...], xq_ref[...]], axis=0)
    xhn = _rms(xh, g).astype(_BF16)
    zh = _dot(xhn, w_in_ref[:, D_MODEL:])
    uh = zh[:, :D_MODEL] * zh[:, D_MODEL:]
    u_scr[0:HALO_C, :] = jnp.where(first, 0.0, uh[:HALO_C])
    u_scr[HALO_C + TM:, :] = jnp.where(last, 0.0, uh[HALO_C:])
    u_scr[HALO_C:HALO_C + TM, :] = z[:, D_MODEL:2 * D_MODEL] * z[:, 2 * D_MODEL:]

    conv = (u_scr[HALO_C - 1:HALO_C - 1 + TM, :] * cw_ref[0:1, :]
            + u_scr[HALO_C:HALO_C + TM, :] * cw_ref[1:2, :]
            + u_scr[HALO_C + 1:HALO_C + 1 + TM, :] * cw_ref[2:3, :])
    y = (z[:, :D_MODEL] * conv).astype(_BF16)
    x1 = x + _dot(y, w_out_ref[...])
    o_ref[...] = x1

    hn = _rms(x1, g2_ref[...])
    hi = hn.astype(_BF16)
    lo = (hn - hi.astype(_F32)).astype(_BF16)
    rhi = rhi_ref[...]
    logits = _dot(hi, rhi) + (_dot(lo, rhi) + _dot(hi, rlo_ref[...]))

    lane = lax.broadcasted_iota(jnp.int32, logits.shape, 1).astype(_F32)
    neg = jnp.float32(-jnp.inf)
    lg = jnp.where(lane < N_EXPERTS, logits, neg)
    m1 = jnp.max(lg, axis=-1, keepdims=True)
    i1 = jnp.min(jnp.where(lg == m1, lane, float(LANES)), axis=-1, keepdims=True)
    lg2 = jnp.where(lane == i1, neg, lg)
    m2 = jnp.max(lg2, axis=-1, keepdims=True)
    i2 = jnp.min(jnp.where(lg2 == m2, lane, float(LANES)), axis=-1, keepdims=True)
    e2 = jnp.exp(m2 - m1)
    den = 1.0 + e2
    route_ref[...] = jnp.where(lane == 0, i1, jnp.where(lane == 1, i2, jnp.where(
        lane == 2, 1.0 / den, jnp.where(lane == 3, e2 / den, 0.0))))


def _mixer_c(x, seq, p):
    n_rows = x.shape[0]
    in_specs = _row_specs(1, n_rows, HALO_C, [0]) + [
        _const_spec((1, D_MODEL)),
        _const_spec((D_MODEL, 3 * D_MODEL)),
        _const_spec((CONV_C_WIDTH, D_MODEL)),
        _const_spec((D_MODEL, D_MODEL)),
        _const_spec((1, D_MODEL)),
        _const_spec((D_MODEL, LANES)), _const_spec((D_MODEL, LANES)),
    ]
    return pl.pallas_call(
        functools.partial(_mixer_c_kernel, seq // TM),
        grid=(n_rows // TM,),
        in_specs=in_specs,
        out_specs=[pl.BlockSpec((TM, D_MODEL), lambda i: (i, 0)),
                   pl.BlockSpec((TM, LANES), lambda i: (i, 0))],
        out_shape=[jax.ShapeDtypeStruct((n_rows, D_MODEL), _F32),
                   jax.ShapeDtypeStruct((n_rows, LANES), _F32)],
        scratch_shapes=[pltpu.VMEM((TM + 2 * HALO_C, D_MODEL), _F32)],
        compiler_params=pltpu.CompilerParams(
            dimension_semantics=("arbitrary",), vmem_limit_bytes=VMEM_LIMIT),
        name="mixer_c",
    )(x, x, x, p["g"], p["w_in"], p["conv_w"], p["w_out"], p["g_ffn"], p["r_hi"], p["r_lo"])


def _moe_kernel(te_ref, nv_ref, nu_ref, src_ref, srcn_ref, dst_ref, gate_ref, x_hbm, g_ref,
                w13_ref, w2_ref, ys_hbm, xbuf, ybuf, acc_ref, gsem, ssem):
    i = pl.program_id(0)
    n_used = nu_ref[0]
    slot = i % 2

    def start_gather(idx_ref, s):
        def body(r, carry):
            pltpu.make_async_copy(x_hbm.at[pl.ds(idx_ref[0, 0, r], 1), :],
                                  xbuf.at[s, pl.ds(r, 1), :], gsem.at[s]).start()
            return carry
        lax.fori_loop(0, TM, body, 0)

    def wait_gather(s):
        pltpu.make_async_copy(x_hbm.at[pl.ds(0, TM), :], xbuf.at[s], gsem.at[s]).wait()

    def wait_scatter(tile):
        s = tile % 2
        n = nv_ref[tile]
        n8 = pl.multiple_of(lax.shift_left(lax.shift_right_logical(n, 3), 3), 8)

        @pl.when(n8 > 0)
        def _():
            pltpu.make_async_copy(ybuf.at[s, pl.ds(0, n8), :], ys_hbm.at[pl.ds(0, n8), :],
                                  ssem.at[s]).wait()

        def body(r, carry):
            pltpu.make_async_copy(ybuf.at[s, pl.ds(0, 1), :], ys_hbm.at[pl.ds(0, 1), :],
                                  ssem.at[s]).wait()
            return carry
        lax.fori_loop(0, n - n8, body, 0)

    @pl.when(i == 0)
    def _():
        start_gather(src_ref, 0)

    @pl.when(i + 1 < n_used)
    def _():
        start_gather(srcn_ref, 1 - slot)

    @pl.when(i < n_used)
    def _():
        wait_gather(slot)
        xn = _rms(xbuf[slot], g_ref[...]).astype(_BF16)
        _swiglu_into(acc_ref, xn, w13_ref.at[0], w2_ref.at[0])

        @pl.when(i >= 2)
        def _():
            wait_scatter(i - 2)

        ybuf[slot] = acc_ref[...] * gate_ref[...]

        def body(r, carry):
            pltpu.make_async_copy(ybuf.at[slot, pl.ds(r, 1), :],
                                  ys_hbm.at[pl.ds(dst_ref[0, 0, r], 1), :], ssem.at[slot]).start()
            return carry
        lax.fori_loop(0, nv_ref[i], body, 0)

    @pl.when(i == pl.num_programs(0) - 1)
    def _():
        wait_scatter(n_used - 2)
        wait_scatter(n_used - 1)


def _moe(x, g, w13, w2, tile_expert, n_valid, n_used, src, dst, gate):
    n_tiles = src.shape[0]
    grid_spec = pltpu.PrefetchScalarGridSpec(
        num_scalar_prefetch=3,
        grid=(n_tiles,),
        in_specs=[
            pl.BlockSpec((1, 1, TM), lambda i, *_: (i, 0, 0), memory_space=pltpu.SMEM),
            pl.BlockSpec((1, 1, TM), lambda i, *_: (jnp.minimum(i + 1, n_tiles - 1), 0, 0),
                         memory_space=pltpu.SMEM),
            pl.BlockSpec((1, 1, TM), lambda i, *_: (i, 0, 0), memory_space=pltpu.SMEM),
            pl.BlockSpec((TM, 1), lambda i, *_: (i, 0)),
            pl.BlockSpec(memory_space=pl.ANY),
            pl.BlockSpec((1, D_MODEL), lambda i, *_: (0, 0)),
            pl.BlockSpec((1, N_FF_CHUNKS, D_MODEL, 2 * FF_CHUNK), lambda i, te, *_: (te[i], 0, 0, 0)),
            pl.BlockSpec((1, N_FF_CHUNKS, FF_CHUNK, D_MODEL), lambda i, te, *_: (te[i], 0, 0, 0)),
        ],
        out_specs=pl.BlockSpec(memory_space=pl.ANY),
        scratch_shapes=[pltpu.VMEM((2, TM, D_MODEL), _F32),
                        pltpu.VMEM((2, TM, D_MODEL), _F32),
                        pltpu.VMEM((TM, D_MODEL), _F32),
                        pltpu.SemaphoreType.DMA((2,)),
                        pltpu.SemaphoreType.DMA((2,))],
    )
    return pl.pallas_call(
        _moe_kernel,
        grid_spec=grid_spec,
        out_shape=jax.ShapeDtypeStruct((TOP_K * x.shape[0], D_MODEL), _F32),
        compiler_params=pltpu.CompilerParams(
            dimension_semantics=("arbitrary",), vmem_limit_bytes=VMEM_LIMIT),
        name="moe",
    )(tile_expert, n_valid, n_used, src, src, dst, gate, x, g, w13, w2)


def _route_plan(route, n_rows):
    n_tiles = (TOP_K * n_rows + N_EXPERTS * (TM - 1)) // TM
    n_sorted = n_tiles * TM
    experts = route[:, :TOP_K].astype(jnp.int32)
    gates = route[:, TOP_K:2 * TOP_K]
    sel = jnp.sum(jax.nn.one_hot(experts, N_EXPERTS, dtype=jnp.int32), axis=1)
    cum = jnp.cumsum(sel, axis=0)
    counts = cum[-1]
    tiles_e = (counts + TM - 1) // TM
    tile_end = jnp.cumsum(tiles_e)
    row_start = (tile_end - tiles_e) * TM
    rank = jnp.take_along_axis(cum - sel, experts, axis=1)
    pos = (row_start[experts] + rank).reshape(-1)
    tok = jnp.arange(n_rows, dtype=jnp.int32)
    src = jnp.zeros((n_sorted,), jnp.int32).at[pos].set(jnp.repeat(tok, TOP_K))
    slot_row = (jnp.arange(TOP_K, dtype=jnp.int32)[None, :] * n_rows + tok[:, None]).reshape(-1)
    dst = jnp.zeros((n_sorted,), jnp.int32).at[pos].set(slot_row)
    gate = jnp.zeros((n_sorted,), _F32).at[pos].set(gates.reshape(-1))
    n_used = tile_end[-1]
    tile_ids = jnp.arange(n_tiles, dtype=jnp.int32)
    tile_expert = jnp.searchsorted(tile_end, jnp.minimum(tile_ids, n_used - 1), side="right")
    tile_expert = tile_expert.astype(jnp.int32)
    rows_left = counts[tile_expert] - (tile_ids - (tile_end - tiles_e)[tile_expert]) * TM
    n_valid = jnp.where(tile_ids < n_used, jnp.clip(rows_left, 0, TM), 0)
    return (tile_expert, n_valid.astype(jnp.int32), n_used.reshape(1).astype(jnp.int32),
            src.reshape(n_tiles, 1, TM), dst.reshape(n_tiles, 1, TM), gate.reshape(n_sorted, 1))


def _final_kernel(x_ref, ya_ref, yb_ref, g_ref, o_ref):
    o_ref[...] = _rms(x_ref[...] + ya_ref[...] + yb_ref[...], g_ref[...])


def _final_norm(x, ys, g):
    n_rows = x.shape[0]
    nt = n_rows // TM
    return pl.pallas_call(
        _final_kernel,
        grid=(nt,),
        in_specs=[pl.BlockSpec((TM, D_MODEL), lambda i: (i, 0)),
                  pl.BlockSpec((TM, D_MODEL), lambda i: (i, 0)),
                  pl.BlockSpec((TM, D_MODEL), lambda i: (i + nt, 0)),
                  _const_spec((1, D_MODEL))],
        out_specs=pl.BlockSpec((TM, D_MODEL), lambda i: (i, 0)),
        out_shape=jax.ShapeDtypeStruct((n_rows, D_MODEL), _F32),
        compiler_params=pltpu.CompilerParams(
            dimension_semantics=("arbitrary",), vmem_limit_bytes=VMEM_LIMIT),
        name="final_norm",
    )(x, ys, ys, g)


def _chunk_w13(w1, w3):
    lead = w1.shape[:-2]
    a = w1.reshape(*lead, D_MODEL, N_FF_CHUNKS, FF_CHUNK)
    b = w3.reshape(*lead, D_MODEL, N_FF_CHUNKS, FF_CHUNK)
    w = jnp.concatenate([a, b], axis=-1).astype(_BF16)
    return jnp.moveaxis(w, -2, -3)


def _chunk_w2(w2):
    lead = w2.shape[:-2]
    return w2.astype(_BF16).reshape(*lead, N_FF_CHUNKS, FF_CHUNK, D_MODEL)


def _row(v):
    return v.reshape(1, -1)


def kernel(x, norm_mix_g, norm_ffn_g, ev_w_in, ev_ws, ev_bs, ev_ln_a_g, ev_ln_a_b, ev_conv_w,
           ev_conv_b, ev_ln_b_g, ev_ln_b_b, ev_w_out, od_w_in, od_conv_w, od_w_out, ffn_w1,
           ffn_w3, ffn_w2, moe_router, moe_w1, moe_w3, moe_w2, final_g):
    bsz, seq, d = x.shape
    n_rows = bsz * seq
    depth = norm_mix_g.shape[0]
    assert d == D_MODEL and seq % TM == 0 and depth % 2 == 0

    h = x.reshape(n_rows, d)
    addends, offsets = [h], [0]
    for layer in range(depth):
        i = layer // 2
        if layer % 2 == 0:
            p = dict(
                g=_row(norm_mix_g[layer]), w_in=ev_w_in[i].astype(_BF16), ws=ev_ws[i].astype(_BF16),
                bsb=jnp.broadcast_to(ev_bs[i][:, :, None], (A_HEADS, CHUNK, LANES)),
                ln_a_g=_row(ev_ln_a_g[i]), ln_a_b=_row(ev_ln_a_b[i]), conv_w=ev_conv_w[i],
                conv_b=_row(ev_conv_b[i]), ln_b_g=_row(ev_ln_b_g[i]), ln_b_b=_row(ev_ln_b_b[i]),
                w_out=ev_w_out[i].astype(_BF16))
            h = _mixer_ab(addends, offsets, n_rows, seq, p)
            h = _ffn(h, _row(norm_ffn_g[layer]), _chunk_w13(ffn_w1[i], ffn_w3[i]), _chunk_w2(ffn_w2[i]))
            addends, offsets = [h], [0]
        else:
            router = jnp.pad(moe_router[i], ((0, 0), (0, LANES - N_EXPERTS)))
            r_hi = router.astype(_BF16)
            r_lo = (router - r_hi.astype(_F32)).astype(_BF16)
            p = dict(g=_row(norm_mix_g[layer]), w_in=od_w_in[i].astype(_BF16), conv_w=od_conv_w[i],
                     w_out=od_w_out[i].astype(_BF16), g_ffn=_row(norm_ffn_g[layer]), r_hi=r_hi, r_lo=r_lo)
            h, route = _mixer_c(h, seq, p)
            ys = _moe(h, _row(norm_ffn_g[layer]), _chunk_w13(moe_w1[i], moe_w3[i]), _chunk_w2(moe_w2[i]),
                      *_route_plan(route, n_rows))
            addends, offsets = [h, ys, ys], [0, 0, n_rows]
    out = _final_norm(addends[0], addends[1], final_g.reshape(1, d))
    return out.reshape(bsz, seq, d)
```

```python
import functools

import jax
import jax.numpy as jnp
from jax import lax
from jax.experimental import pallas as pl
from jax.experimental.pallas import tpu as pltpu

D_MODEL = 1024
A_WIDTH = 512
A_HEADS = 4
CHUNK = 128
B_WIDTH = 512
CONV_B_WIDTH = 31
CONV_B_PAD = 15
CONV_C_WIDTH = 3
D_FF = 2816
N_EXPERTS = 8
TOP_K = 2
RMS_EPS = 1e-6
LN_EPS = 1e-5

LANES = 128
TM = 512
HALO_B = 16
HALO_C = 8
FF_SPLITS = (0, 512, 1024, 1536, 2048, 2560, D_FF)
CONV_ROWS = 64
PLAN_ROWS = 256
VMEM_LIMIT = 60 * 1024 * 1024
GATE_LANE = TOP_K

_F32 = jnp.float32
_BF16 = jnp.bfloat16


def _rms(x, g):
    y = x * lax.rsqrt(jnp.mean(x * x, axis=-1, keepdims=True) + RMS_EPS)
    return y * g


def _layernorm(x, g, b):
    mu = jnp.mean(x, axis=-1, keepdims=True)
    xc = x - mu
    y = xc * lax.rsqrt(jnp.mean(xc * xc, axis=-1, keepdims=True) + LN_EPS)
    return y * g + b


def _sigmoid(x):
    return 1.0 / (1.0 + jnp.exp(-x))


def _gelu_tanh(x):
    c = 0.7978845608028654
    return 0.5 * x * (1.0 + jnp.tanh(c * (x + 0.044715 * (x * x * x))))


def _dot(a, b):
    return jnp.dot(a, b, preferred_element_type=_F32)


def _residual_in(refs):
    if len(refs) == 1:
        return refs[0][...]
    x_ref, ya_ref, yb_ref, route_ref = refs
    r = route_ref[...]
    return (x_ref[...] + r[:, GATE_LANE:GATE_LANE + 1] * ya_ref[...]
            + r[:, GATE_LANE + 1:GATE_LANE + 2] * yb_ref[...])


def _mixer_ab_kernel(n_in, tiles_per_seq, *refs):
    mains = refs[0:n_in]
    prevs = refs[n_in:2 * n_in]
    nexts = refs[2 * n_in:3 * n_in]
    (g_ref, w_in_ref, ws_ref, bsb_ref, lnag_ref, lnab_ref, cw_ref, cb_ref,
     lnbg_ref, lnbb_ref, w_out_ref, o_ref, glu_scr, mix_scr) = refs[3 * n_in:]

    i = pl.program_id(0)
    first = (i % tiles_per_seq) == 0
    last = (i % tiles_per_seq) == tiles_per_seq - 1

    x = _residual_in(mains)
    g = g_ref[...]
    xn = _rms(x, g).astype(_BF16)
    z = _dot(xn, w_in_ref[...])

    xh = jnp.concatenate([_residual_in(prevs), _residual_in(nexts)], axis=0)
    xhn = _rms(xh, g).astype(_BF16)
    zh = _dot(xhn, w_in_ref[:, 2 * A_WIDTH:])
    glu_h = zh[:, :B_WIDTH] * _sigmoid(zh[:, B_WIDTH:])
    glu_scr[0:HALO_B, :] = jnp.where(first, 0.0, glu_h[:HALO_B])
    glu_scr[HALO_B + TM:, :] = jnp.where(last, 0.0, glu_h[HALO_B:])
    glu_scr[HALO_B:HALO_B + TM, :] = (
        z[:, 2 * A_WIDTH:2 * A_WIDTH + B_WIDTH] * _sigmoid(z[:, 2 * A_WIDTH + B_WIDTH:]))

    a_u = _gelu_tanh(z[:, :A_WIDTH])
    a_v = _layernorm(_gelu_tanh(z[:, A_WIDTH:2 * A_WIDTH]), lnag_ref[...], lnab_ref[...])
    a_vb = a_v.astype(_BF16)
    for c in range(TM // CHUNK):
        rows = slice(c * CHUNK, (c + 1) * CHUNK)
        for h in range(A_HEADS):
            cols = slice(h * LANES, (h + 1) * LANES)
            v = _dot(ws_ref[h], a_vb[rows, cols]) + bsb_ref[h]
            mix_scr[rows, cols] = (a_u[rows, cols] * v).astype(_BF16)

    cb = cb_ref[...]
    for r in range(TM // CONV_ROWS):
        base = r * CONV_ROWS
        acc = jnp.zeros((CONV_ROWS, B_WIDTH), _F32)
        for k in range(CONV_B_WIDTH):
            off = base + HALO_B - CONV_B_PAD + k
            acc = acc + glu_scr[off:off + CONV_ROWS, :] * cw_ref[k:k + 1, :]
        y = _layernorm(acc + cb, lnbg_ref[...], lnbb_ref[...])
        mix_scr[base:base + CONV_ROWS, A_WIDTH:] = (y * _sigmoid(y)).astype(_BF16)

    o_ref[...] = x + _dot(mix_scr[...], w_out_ref[...])


def _const_spec(shape):
    nd = len(shape)
    return pl.BlockSpec(shape, lambda *_: (0,) * nd, pipeline_mode=pl.Buffered(1))


def _tile_specs(widths_and_offsets, n_rows, halo):
    nb = n_rows // halo
    per = TM // halo
    mains, prevs, nexts = [], [], []
    for width, off in widths_and_offsets:
        mains.append(pl.BlockSpec((TM, width), lambda i, o=off: (i + o // TM, 0)))
        prevs.append(pl.BlockSpec(
            (halo, width), lambda i, o=off: (jnp.maximum(i * per - 1, 0) + o // halo, 0)))
        nexts.append(pl.BlockSpec(
            (halo, width), lambda i, o=off: (jnp.minimum((i + 1) * per, nb - 1) + o // halo, 0)))
    return mains + prevs + nexts


def _layer_inputs(h, moe_out, n_rows):
    if moe_out is None:
        return [h], [(D_MODEL, 0)]
    ys, route = moe_out
    return [h, ys, ys, route], [(D_MODEL, 0), (D_MODEL, 0), (D_MODEL, n_rows), (LANES, 0)]


def _mixer_ab(h, moe_out, n_rows, seq, p):
    arrays, descr = _layer_inputs(h, moe_out, n_rows)
    in_specs = _tile_specs(descr, n_rows, HALO_B) + [
        _const_spec((1, D_MODEL)),
        _const_spec((D_MODEL, 2 * A_WIDTH + 2 * B_WIDTH)),
        _const_spec((A_HEADS, CHUNK, CHUNK)),
        _const_spec((A_HEADS, CHUNK, LANES)),
        _const_spec((1, A_WIDTH)), _const_spec((1, A_WIDTH)),
        _const_spec((CONV_B_WIDTH, B_WIDTH)), _const_spec((1, B_WIDTH)),
        _const_spec((1, B_WIDTH)), _const_spec((1, B_WIDTH)),
        _const_spec((A_WIDTH + B_WIDTH, D_MODEL)),
    ]
    return pl.pallas_call(
        functools.partial(_mixer_ab_kernel, len(arrays), seq // TM),
        grid=(n_rows // TM,),
        in_specs=in_specs,
        out_specs=pl.BlockSpec((TM, D_MODEL), lambda i: (i, 0)),
        out_shape=jax.ShapeDtypeStruct((n_rows, D_MODEL), _F32),
        scratch_shapes=[pltpu.VMEM((TM + 2 * HALO_B, B_WIDTH), _F32),
                        pltpu.VMEM((TM, A_WIDTH + B_WIDTH), _BF16)],
        compiler_params=pltpu.CompilerParams(
            dimension_semantics=("arbitrary",), vmem_limit_bytes=VMEM_LIMIT),
        name="mixer_ab",
    )(*(arrays * 3), p["g"], p["w_in"], p["ws"], p["bsb"], p["ln_a_g"], p["ln_a_b"],
      p["conv_w"], p["conv_b"], p["ln_b_g"], p["ln_b_b"], p["w_out"])


def _swiglu(xn_ref, w1_ref, w3_ref, w2_ref):
    acc = None
    for lo, hi in zip(FF_SPLITS[:-1], FF_SPLITS[1:]):
        h1 = _dot(xn_ref[...], w1_ref[:, lo:hi])
        h3 = _dot(xn_ref[...], w3_ref[:, lo:hi])
        a = (h1 * _sigmoid(h1) * h3).astype(_BF16)
        d = _dot(a, w2_ref[lo:hi, :])
        acc = d if acc is None else acc + d
    return acc


def _ffn_kernel(x_ref, g_ref, w1_ref, w3_ref, w2_ref, o_ref, xn_ref):
    x = x_ref[...]
    xn_ref[...] = _rms(x, g_ref[...]).astype(_BF16)
    o_ref[...] = x + _swiglu(xn_ref, w1_ref, w3_ref, w2_ref)


def _ffn(x, g, w1, w3, w2):
    n_rows = x.shape[0]
    return pl.pallas_call(
        _ffn_kernel,
        grid=(n_rows // TM,),
        in_specs=[pl.BlockSpec((TM, D_MODEL), lambda i: (i, 0)),
                  _const_spec((1, D_MODEL)),
                  _const_spec((D_MODEL, D_FF)), _const_spec((D_MODEL, D_FF)),
                  _const_spec((D_FF, D_MODEL))],
        out_specs=pl.BlockSpec((TM, D_MODEL), lambda i: (i, 0)),
        out_shape=jax.ShapeDtypeStruct((n_rows, D_MODEL), _F32),
        scratch_shapes=[pltpu.VMEM((TM, D_MODEL), _BF16)],
        compiler_params=pltpu.CompilerParams(
            dimension_semantics=("arbitrary",), vmem_limit_bytes=VMEM_LIMIT),
        name="ffn",
    )(x, g, w1, w3, w2)


def _mixer_c_kernel(tiles_per_seq, x_ref, xp_ref, xq_ref, g_ref, w_in_ref, cw_ref, w_out_ref,
                    g2_ref, rhi_ref, rlo_ref, o_ref, route_ref, u_scr):
    i = pl.program_id(0)
    first = (i % tiles_per_seq) == 0
    last = (i % tiles_per_seq) == tiles_per_seq - 1

    x = x_ref[...]
    g = g_ref[...]
    xn = _rms(x, g).astype(_BF16)
    z = _dot(xn, w_in_ref[...])

    xh = jnp.concatenate([xp_ref[...], xq_ref[...]], axis=0)
    xhn = _rms(xh, g).astype(_BF16)
    zh = _dot(xhn, w_in_ref[:, D_MODEL:])
    uh = zh[:, :D_MODEL] * zh[:, D_MODEL:]
    u_scr[0:HALO_C, :] = jnp.where(first, 0.0, uh[:HALO_C])
    u_scr[HALO_C + TM:, :] = jnp.where(last, 0.0, uh[HALO_C:])
    u_scr[HALO_C:HALO_C + TM, :] = z[:, D_MODEL:2 * D_MODEL] * z[:, 2 * D_MODEL:]

    conv = (u_scr[HALO_C - 1:HALO_C - 1 + TM, :] * cw_ref[0:1, :]
            + u_scr[HALO_C:HALO_C + TM, :] * cw_ref[1:2, :]
            + u_scr[HALO_C + 1:HALO_C + 1 + TM, :] * cw_ref[2:3, :])
    y = (z[:, :D_MODEL] * conv).astype(_BF16)
    x1 = x + _dot(y, w_out_ref[...])
    o_ref[...] = x1

    hn = _rms(x1, g2_ref[...])
    hi = hn.astype(_BF16)
    lo = (hn - hi.astype(_F32)).astype(_BF16)
    rhi = rhi_ref[...]
    logits = _dot(hi, rhi) + (_dot(lo, rhi) + _dot(hi, rlo_ref[...]))

    lane = lax.broadcasted_iota(jnp.int32, logits.shape, 1).astype(_F32)
    neg = jnp.float32(-jnp.inf)
    lg = jnp.where(lane < N_EXPERTS, logits, neg)
    m1 = jnp.max(lg, axis=-1, keepdims=True)
    i1 = jnp.min(jnp.where(lg == m1, lane, float(LANES)), axis=-1, keepdims=True)
    lg2 = jnp.where(lane == i1, neg, lg)
    m2 = jnp.max(lg2, axis=-1, keepdims=True)
    i2 = jnp.min(jnp.where(lg2 == m2, lane, float(LANES)), axis=-1, keepdims=True)
    e2 = jnp.exp(m2 - m1)
    den = 1.0 + e2
    route_ref[...] = jnp.where(lane == 0, i1, jnp.where(lane == 1, i2, jnp.where(
        lane == GATE_LANE, 1.0 / den, jnp.where(lane == GATE_LANE + 1, e2 / den, 0.0))))


def _mixer_c(x, seq, p):
    n_rows = x.shape[0]
    in_specs = _tile_specs([(D_MODEL, 0)], n_rows, HALO_C) + [
        _const_spec((1, D_MODEL)),
        _const_spec((D_MODEL, 3 * D_MODEL)),
        _const_spec((CONV_C_WIDTH, D_MODEL)),
        _const_spec((D_MODEL, D_MODEL)),
        _const_spec((1, D_MODEL)),
        _const_spec((D_MODEL, LANES)), _const_spec((D_MODEL, LANES)),
    ]
    return pl.pallas_call(
        functools.partial(_mixer_c_kernel, seq // TM),
        grid=(n_rows // TM,),
        in_specs=in_specs,
        out_specs=[pl.BlockSpec((TM, D_MODEL), lambda i: (i, 0)),
                   pl.BlockSpec((TM, LANES), lambda i: (i, 0))],
        out_shape=[jax.ShapeDtypeStruct((n_rows, D_MODEL), _F32),
                   jax.ShapeDtypeStruct((n_rows, LANES), _F32)],
        scratch_shapes=[pltpu.VMEM((TM + 2 * HALO_C, D_MODEL), _F32)],
        compiler_params=pltpu.CompilerParams(
            dimension_semantics=("arbitrary",), vmem_limit_bytes=VMEM_LIMIT),
        name="mixer_c",
    )(x, x, x, p["g"], p["w_in"], p["conv_w"], p["w_out"], p["g_ffn"], p["r_hi"], p["r_lo"])


def _plan_kernel(te_ref, r0_ref, lo_ref, hi_ref, cum_ref, u_ref):
    del te_ref
    i = pl.program_id(0)
    n_pairs = cum_ref.shape[0] * LANES
    lo = lo_ref[i]
    parts = []
    for half in range(TM // PLAN_ROWS):
        rank = r0_ref[i] + half * PLAN_ROWS + lax.broadcasted_iota(jnp.int32, (PLAN_ROWS, LANES), 0)

        def body(b, acc):
            return acc + jnp.where(cum_ref[b] <= rank, 1.0, 0.0)

        acc = lax.fori_loop(lo, hi_ref[i], body, jnp.zeros((PLAN_ROWS, LANES), _F32))
        parts.append(jnp.sum(acc.T, axis=0, keepdims=True))
    cnt = jnp.concatenate(parts, axis=1).astype(jnp.int32) + lo * LANES
    u_ref[0] = jnp.minimum(cnt, n_pairs - 1)


def _route_plan(route, n_rows):
    n_pairs = TOP_K * n_rows
    n_blocks = n_pairs // LANES
    n_tiles = (n_pairs + N_EXPERTS * (TM - 1)) // TM
    e_cat = jnp.concatenate([route[:, k] for k in range(TOP_K)]).astype(jnp.int32)
    onehot = (e_cat[None, :] == jnp.arange(N_EXPERTS, dtype=jnp.int32)[:, None]).astype(jnp.int32)
    cum = jnp.cumsum(onehot, axis=1)
    counts = cum[:, -1]
    tiles_e = (counts + TM - 1) // TM
    tile_end = jnp.cumsum(tiles_e)
    n_used = tile_end[-1]
    tile_ids = jnp.arange(n_tiles, dtype=jnp.int32)
    tile_expert = jnp.sum(
        (tile_end[None, :] <= jnp.minimum(tile_ids, n_used - 1)[:, None]).astype(jnp.int32), axis=1)
    r0 = (tile_ids - (tile_end - tiles_e)[tile_expert]) * TM
    n_valid = jnp.where(tile_ids < n_used, jnp.clip(counts[tile_expert] - r0, 0, TM), 0)
    cum_blocks = cum.reshape(N_EXPERTS, n_blocks, LANES)
    blk_lo = jnp.sum((cum_blocks[tile_expert, :, -1] <= r0[:, None]).astype(jnp.int32), axis=1)
    blk_hi = jnp.sum((cum_blocks[tile_expert, :, 0] <= (r0 + TM - 1)[:, None]).astype(jnp.int32), axis=1)

    u = pl.pallas_call(
        _plan_kernel,
        grid_spec=pltpu.PrefetchScalarGridSpec(
            num_scalar_prefetch=4,
            grid=(n_tiles,),
            in_specs=[pl.BlockSpec((n_blocks, 1, LANES), lambda i, te, *_: (te[i], 0, 0))],
            out_specs=pl.BlockSpec((1, 1, TM), lambda i, *_: (i, 0, 0)),
        ),
        out_shape=jax.ShapeDtypeStruct((n_tiles, 1, TM), jnp.int32),
        compiler_params=pltpu.CompilerParams(
            dimension_semantics=("arbitrary",), vmem_limit_bytes=VMEM_LIMIT),
        name="plan",
    )(tile_expert, r0, blk_lo, blk_hi, cum.reshape(N_EXPERTS * n_blocks, 1, LANES))
    src = jnp.where(u >= n_rows, u - n_rows, u)
    return tile_expert, n_valid, n_used.reshape(1), src, u


def _moe_kernel(te_ref, nv_ref, nu_ref, src_ref, srcn_ref, dst_ref, x_hbm, g_ref,
                w1_ref, w3_ref, w2_ref, ys_hbm, xbuf, ybuf, xn_ref, gsem, ssem):
    del te_ref
    i = pl.program_id(0)
    n_used = nu_ref[0]
    slot = i % 2

    def start_gather(idx_ref, s):
        def body(r, carry):
            pltpu.make_async_copy(x_hbm.at[pl.ds(idx_ref[0, 0, r], 1), :],
                                  xbuf.at[s, pl.ds(r, 1), :], gsem.at[s]).start()
            return carry
        lax.fori_loop(0, TM, body, 0)

    def wait_gather(s):
        pltpu.make_async_copy(x_hbm.at[pl.ds(0, TM), :], xbuf.at[s], gsem.at[s]).wait()

    def wait_scatter(tile):
        s = tile % 2
        n = nv_ref[tile]
        n8 = pl.multiple_of(lax.shift_left(lax.shift_right_logical(n, 3), 3), 8)

        @pl.when(n8 > 0)
        def _():
            pltpu.make_async_copy(ybuf.at[s, pl.ds(0, n8), :], ys_hbm.at[pl.ds(0, n8), :],
                                  ssem.at[s]).wait()

        def body(r, carry):
            pltpu.make_async_copy(ybuf.at[s, pl.ds(0, 1), :], ys_hbm.at[pl.ds(0, 1), :],
                                  ssem.at[s]).wait()
            return carry
        lax.fori_loop(0, n - n8, body, 0)

    @pl.when(i == 0)
    def _():
        start_gather(src_ref, 0)

    @pl.when(i + 1 < n_used)
    def _():
        start_gather(srcn_ref, 1 - slot)

    @pl.when(i < n_used)
    def _():
        wait_gather(slot)
        xn_ref[...] = _rms(xbuf[slot], g_ref[...]).astype(_BF16)
        y = _swiglu(xn_ref, w1_ref.at[0], w3_ref.at[0], w2_ref.at[0])

        @pl.when(i >= 2)
        def _():
            wait_scatter(i - 2)

        ybuf[slot] = y

        def body(r, carry):
            pltpu.make_async_copy(ybuf.at[slot, pl.ds(r, 1), :],
                                  ys_hbm.at[pl.ds(dst_ref[0, 0, r], 1), :], ssem.at[slot]).start()
            return carry
        lax.fori_loop(0, nv_ref[i], body, 0)

    @pl.when(i == pl.num_programs(0) - 1)
    def _():
        wait_scatter(n_used - 2)
        wait_scatter(n_used - 1)


def _moe(x, g, w1, w3, w2, tile_expert, n_valid, n_used, src, dst):
    n_tiles = src.shape[0]
    grid_spec = pltpu.PrefetchScalarGridSpec(
        num_scalar_prefetch=3,
        grid=(n_tiles,),
        in_specs=[
            pl.BlockSpec((1, 1, TM), lambda i, *_: (i, 0, 0), memory_space=pltpu.SMEM),
            pl.BlockSpec((1, 1, TM), lambda i, *_: (jnp.minimum(i + 1, n_tiles - 1), 0, 0),
                         memory_space=pltpu.SMEM),
            pl.BlockSpec((1, 1, TM), lambda i, *_: (i, 0, 0), memory_space=pltpu.SMEM),
            pl.BlockSpec(memory_space=pl.ANY),
            pl.BlockSpec((1, D_MODEL), lambda i, *_: (0, 0)),
            pl.BlockSpec((1, D_MODEL, D_FF), lambda i, te, *_: (te[i], 0, 0)),
            pl.BlockSpec((1, D_MODEL, D_FF), lambda i, te, *_: (te[i], 0, 0)),
            pl.BlockSpec((1, D_FF, D_MODEL), lambda i, te, *_: (te[i], 0, 0)),
        ],
        out_specs=pl.BlockSpec(memory_space=pl.ANY),
        scratch_shapes=[pltpu.VMEM((2, TM, D_MODEL), _F32),
                        pltpu.VMEM((2, TM, D_MODEL), _F32),
                        pltpu.VMEM((TM, D_MODEL), _BF16),
                        pltpu.SemaphoreType.DMA((2,)),
                        pltpu.SemaphoreType.DMA((2,))],
    )
    return pl.pallas_call(
        _moe_kernel,
        grid_spec=grid_spec,
        out_shape=jax.ShapeDtypeStruct((TOP_K * x.shape[0], D_MODEL), _F32),
        compiler_params=pltpu.CompilerParams(
            dimension_semantics=("arbitrary",), vmem_limit_bytes=VMEM_LIMIT),
        name="moe",
    )(tile_expert, n_valid, n_used, src, src, dst, x, g, w1, w3, w2)


def _final_kernel(*refs):
    *in_refs, g_ref, o_ref = refs
    o_ref[...] = _rms(_residual_in(in_refs), g_ref[...])


def _final_norm(h, moe_out, g):
    n_rows = h.shape[0]
    arrays, descr = _layer_inputs(h, moe_out, n_rows)
    return pl.pallas_call(
        _final_kernel,
        grid=(n_rows // TM,),
        in_specs=[pl.BlockSpec((TM, w), lambda i, o=off: (i + o // TM, 0)) for w, off in descr]
        + [_const_spec((1, D_MODEL))],
        out_specs=pl.BlockSpec((TM, D_MODEL), lambda i: (i, 0)),
        out_shape=jax.ShapeDtypeStruct((n_rows, D_MODEL), _F32),
        compiler_params=pltpu.CompilerParams(
            dimension_semantics=("arbitrary",), vmem_limit_bytes=VMEM_LIMIT),
        name="final_norm",
    )(*arrays, g)


def _row(v):
    return v.reshape(1, -1)


def kernel(x, norm_mix_g, norm_ffn_g, ev_w_in, ev_ws, ev_bs, ev_ln_a_g, ev_ln_a_b, ev_conv_w,
           ev_conv_b, ev_ln_b_g, ev_ln_b_b, ev_w_out, od_w_in, od_conv_w, od_w_out, ffn_w1,
           ffn_w3, ffn_w2, moe_router, moe_w1, moe_w3, moe_w2, final_g):
    bsz, seq, d = x.shape
    n_rows = bsz * seq
    depth = norm_mix_g.shape[0]
    assert d == D_MODEL and seq % TM == 0 and depth % 2 == 0 and TOP_K == 2

    h = x.reshape(n_rows, d)
    moe_out = None
    for layer in range(depth):
        i = layer // 2
        if layer % 2 == 0:
            p = dict(
                g=_row(norm_mix_g[layer]), w_in=ev_w_in[i].astype(_BF16), ws=ev_ws[i].astype(_BF16),
                bsb=jnp.broadcast_to(ev_bs[i][:, :, None], (A_HEADS, CHUNK, LANES)),
                ln_a_g=_row(ev_ln_a_g[i]), ln_a_b=_row(ev_ln_a_b[i]), conv_w=ev_conv_w[i],
                conv_b=_row(ev_conv_b[i]), ln_b_g=_row(ev_ln_b_g[i]), ln_b_b=_row(ev_ln_b_b[i]),
                w_out=ev_w_out[i].astype(_BF16))
            h = _mixer_ab(h, moe_out, n_rows, seq, p)
            h = _ffn(h, _row(norm_ffn_g[layer]), ffn_w1[i].astype(_BF16), ffn_w3[i].astype(_BF16),
                     ffn_w2[i].astype(_BF16))
            moe_out = None
        else:
            router = jnp.pad(moe_router[i], ((0, 0), (0, LANES - N_EXPERTS)))
            r_hi = router.astype(_BF16)
            r_lo = (router - r_hi.astype(_F32)).astype(_BF16)
            p = dict(g=_row(norm_mix_g[layer]), w_in=od_w_in[i].astype(_BF16), conv_w=od_conv_w[i],
                     w_out=od_w_out[i].astype(_BF16), g_ffn=_row(norm_ffn_g[layer]), r_hi=r_hi, r_lo=r_lo)
            h, route = _mixer_c(h, seq, p)
            ys = _moe(h, _row(norm_ffn_g[layer]), moe_w1[i].astype(_BF16), moe_w3[i].astype(_BF16),
                      moe_w2[i].astype(_BF16), *_route_plan(route, n_rows))
            moe_out = (ys, route)
    out = _final_norm(h, moe_out, _row(final_g))
    return out.reshape(bsz, seq, d)
```

```python
import functools

import jax
import jax.numpy as jnp
from jax import lax
from jax.experimental import pallas as pl
from jax.experimental.pallas import tpu as pltpu

D_MODEL = 1024
A_WIDTH = 512
A_HEADS = 4
CHUNK = 128
B_WIDTH = 512
CONV_B_WIDTH = 31
CONV_B_PAD = 15
CONV_C_WIDTH = 3
D_FF = 2816
N_EXPERTS = 8
TOP_K = 2
RMS_EPS = 1e-6
LN_EPS = 1e-5

LANES = 128
SUBLANES = 8
MXU_N = 256
D_BLOCKS = D_MODEL // LANES
assert D_BLOCKS == SUBLANES
TM = 512
HALO_B = 16
HALO_C = 8
FF_SPLITS = (0, 512, 1024, 1536, 2048, 2560, D_FF)
CONV_ROWS = 64
VMEM_LIMIT = 60 * 1024 * 1024
GATE_LANE = TOP_K

_F32 = jnp.float32
_BF16 = jnp.bfloat16


def _rms(x, g):
    y = x * lax.rsqrt(jnp.mean(x * x, axis=-1, keepdims=True) + RMS_EPS)
    return y * g


def _layernorm(x, g, b):
    mu = jnp.mean(x, axis=-1, keepdims=True)
    xc = x - mu
    y = xc * lax.rsqrt(jnp.mean(xc * xc, axis=-1, keepdims=True) + LN_EPS)
    return y * g + b


def _sigmoid(x):
    return 1.0 / (1.0 + jnp.exp(-x))


def _gelu_tanh(x):
    c = 0.7978845608028654
    return 0.5 * x * (1.0 + jnp.tanh(c * (x + 0.044715 * (x * x * x))))


def _dot(a, b):
    return jnp.dot(a, b, preferred_element_type=_F32)


def _residual_in(refs):
    if len(refs) == 1:
        return refs[0][...]
    x, ya, yb = (ref[...].reshape(ref.shape[0], D_MODEL) for ref in refs[:3])
    r = refs[3][...]
    return x + r[:, GATE_LANE:GATE_LANE + 1] * ya + r[:, GATE_LANE + 1:GATE_LANE + 2] * yb


def _mixer_ab_kernel(n_in, tiles_per_seq, *refs):
    mains = refs[0:n_in]
    prevs = refs[n_in:2 * n_in]
    nexts = refs[2 * n_in:3 * n_in]
    (g_ref, w_in_f32, ws_f32, bsb_ref, lnag_ref, lnab_ref, cw_ref, cb_ref,
     lnbg_ref, lnbb_ref, w_out_f32, o_ref, glu_scr, shift_scr, mix_scr,
     w_in_ref, ws_ref, w_out_ref) = refs[3 * n_in:]

    i = pl.program_id(0)
    first = (i % tiles_per_seq) == 0
    last = (i % tiles_per_seq) == tiles_per_seq - 1

    @pl.when(i == 0)
    def _():
        w_in_ref[...] = w_in_f32[0].astype(_BF16)
        ws_ref[...] = ws_f32[0].astype(_BF16)
        w_out_ref[...] = w_out_f32[0].astype(_BF16)

    x = _residual_in(mains)
    g = g_ref[...]
    xn = _rms(x, g).astype(_BF16)
    z = _dot(xn, w_in_ref[...])

    xh = jnp.concatenate([_residual_in(prevs), _residual_in(nexts)], axis=0)
    xhn = _rms(xh, g).astype(_BF16)
    zh = _dot(xhn, w_in_ref[:, 2 * A_WIDTH:])
    glu_h = zh[:, :B_WIDTH] * _sigmoid(zh[:, B_WIDTH:])
    glu_scr[0:HALO_B, :] = jnp.where(first, 0.0, glu_h[:HALO_B])
    glu_scr[HALO_B + TM:, :] = jnp.where(last, 0.0, glu_h[HALO_B:])
    glu_scr[HALO_B:HALO_B + TM, :] = (
        z[:, 2 * A_WIDTH:2 * A_WIDTH + B_WIDTH] * _sigmoid(z[:, 2 * A_WIDTH + B_WIDTH:]))

    a_u = _gelu_tanh(z[:, :A_WIDTH])
    a_v = _layernorm(_gelu_tanh(z[:, A_WIDTH:2 * A_WIDTH]), lnag_ref[...], lnab_ref[...])
    a_vb = a_v.astype(_BF16)
    for c in range(TM // CHUNK):
        rows = slice(c * CHUNK, (c + 1) * CHUNK)
        for h in range(A_HEADS):
            cols = slice(h * LANES, (h + 1) * LANES)
            v = _dot(ws_ref[h], a_vb[rows, cols]) + bsb_ref[h]
            mix_scr[rows, cols] = (a_u[rows, cols] * v).astype(_BF16)

    n_shift_rows = TM + 2 * HALO_B - SUBLANES
    for s in range(1, SUBLANES):
        shift_scr[s - 1] = glu_scr[s:s + n_shift_rows, :]
    cb = cb_ref[...]
    for r in range(TM // CONV_ROWS):
        base = r * CONV_ROWS
        acc = jnp.zeros((CONV_ROWS, B_WIDTH), _F32)
        for k in range(CONV_B_WIDTH):
            off = base + HALO_B - CONV_B_PAD + k
            s = off % SUBLANES
            src = glu_scr if s == 0 else shift_scr.at[s - 1]
            acc = acc + src[off - s:off - s + CONV_ROWS, :] * cw_ref[k:k + 1, :]
        y = _layernorm(acc + cb, lnbg_ref[...], lnbb_ref[...])
        mix_scr[base:base + CONV_ROWS, A_WIDTH:] = (y * _sigmoid(y)).astype(_BF16)

    o_ref[...] = x + _dot(mix_scr[...], w_out_ref[...])


def _const_spec(shape):
    nd = len(shape)
    return pl.BlockSpec(shape, lambda *_: (0,) * nd, pipeline_mode=pl.Buffered(1))


def _layer_spec(shape, idx):
    nd = len(shape)
    return pl.BlockSpec((1, *shape), lambda *_: (idx,) + (0,) * nd, pipeline_mode=pl.Buffered(1))


ROW = (D_MODEL,)
ROW_TILE = (D_BLOCKS, LANES)


def _tile_specs(rows_and_offsets, n_rows, halo):
    nb = n_rows // halo
    per = TM // halo
    mains, prevs, nexts = [], [], []
    for row, off in rows_and_offsets:
        zeros = (0,) * len(row)
        mains.append(pl.BlockSpec((TM, *row), lambda i, o=off, z=zeros: (i + o // TM, *z)))
        prevs.append(pl.BlockSpec(
            (halo, *row), lambda i, o=off, z=zeros: (jnp.maximum(i * per - 1, 0) + o // halo, *z)))
        nexts.append(pl.BlockSpec(
            (halo, *row),
            lambda i, o=off, z=zeros: (jnp.minimum((i + 1) * per, nb - 1) + o // halo, *z)))
    return mains + prevs + nexts


def _layer_inputs(h, moe_out, n_rows):
    if moe_out is None:
        return [h], [(ROW, 0)]
    ys, route = moe_out
    return [h, ys, ys, route], [(ROW_TILE, 0), (ROW_TILE, 0), (ROW_TILE, n_rows), ((LANES,), 0)]


def _mixer_ab(h, moe_out, n_rows, seq, idx, p):
    arrays, descr = _layer_inputs(h, moe_out, n_rows)
    in_specs = _tile_specs(descr, n_rows, HALO_B) + [
        _const_spec((1, D_MODEL)),
        _layer_spec((D_MODEL, 2 * A_WIDTH + 2 * B_WIDTH), idx),
        _layer_spec((A_HEADS, CHUNK, CHUNK), idx),
        _const_spec((A_HEADS, CHUNK, LANES)),
        _const_spec((1, A_WIDTH)), _const_spec((1, A_WIDTH)),
        _const_spec((CONV_B_WIDTH, B_WIDTH)), _const_spec((1, B_WIDTH)),
        _const_spec((1, B_WIDTH)), _const_spec((1, B_WIDTH)),
        _layer_spec((A_WIDTH + B_WIDTH, D_MODEL), idx),
    ]
    return pl.pallas_call(
        functools.partial(_mixer_ab_kernel, len(arrays), seq // TM),
        grid=(n_rows // TM,),
        in_specs=in_specs,
        out_specs=pl.BlockSpec((TM, D_MODEL), lambda i: (i, 0)),
        out_shape=jax.ShapeDtypeStruct((n_rows, D_MODEL), _F32),
        scratch_shapes=[pltpu.VMEM((TM + 2 * HALO_B, B_WIDTH), _F32),
                        pltpu.VMEM((SUBLANES - 1, TM + 2 * HALO_B - SUBLANES, B_WIDTH), _F32),
                        pltpu.VMEM((TM, A_WIDTH + B_WIDTH), _BF16),
                        pltpu.VMEM((D_MODEL, 2 * A_WIDTH + 2 * B_WIDTH), _BF16),
                        pltpu.VMEM((A_HEADS, CHUNK, CHUNK), _BF16),
                        pltpu.VMEM((A_WIDTH + B_WIDTH, D_MODEL), _BF16)],
        compiler_params=pltpu.CompilerParams(
            dimension_semantics=("arbitrary",), vmem_limit_bytes=VMEM_LIMIT),
        name="mixer_ab",
    )(*(arrays * 3), p["g"], p["w_in"], p["ws"], p["bsb"], p["ln_a_g"], p["ln_a_b"],
      p["conv_w"], p["conv_b"], p["ln_b_g"], p["ln_b_b"], p["w_out"])


def _swiglu_pieces():
    n = 0
    for lo, hi in zip(FF_SPLITS[:-1], FF_SPLITS[1:]):
        n += 2 * ((hi - lo) // MXU_N) + D_MODEL // MXU_N
    return n


def _swiglu(xn_ref, w1_ref, w3_ref, w2_ref, side_work=None):
    piece = [0]

    def dot_cols(a, w_ref, rows, lo, hi):
        outs = []
        for c in range(lo, hi, MXU_N):
            if side_work is not None:
                side_work(piece[0])
            piece[0] += 1
            outs.append(_dot(a(), w_ref[rows, c:c + MXU_N].astype(_BF16)))
        return outs

    acc = None
    for lo, hi in zip(FF_SPLITS[:-1], FF_SPLITS[1:]):
        h1 = jnp.concatenate(dot_cols(lambda: xn_ref[...], w1_ref, slice(None), lo, hi), axis=1)
        h3 = jnp.concatenate(dot_cols(lambda: xn_ref[...], w3_ref, slice(None), lo, hi), axis=1)
        a = (h1 * _sigmoid(h1) * h3).astype(_BF16)
        d = dot_cols(lambda: a, w2_ref, slice(lo, hi), 0, D_MODEL)
        acc = d if acc is None else [p + q for p, q in zip(acc, d)]
    return acc


def _ffn_kernel(x_ref, g_ref, w1_ref, w3_ref, w2_ref, o_ref, xn_ref):
    x = x_ref[...]
    xn_ref[...] = _rms(x, g_ref[...]).astype(_BF16)
    y = _swiglu(xn_ref, w1_ref.at[0], w3_ref.at[0], w2_ref.at[0])
    o_ref[...] = x + jnp.concatenate(y, axis=1)


def _ffn(x, g, idx, w1, w3, w2):
    n_rows = x.shape[0]
    return pl.pallas_call(
        _ffn_kernel,
        grid=(n_rows // TM,),
        in_specs=[pl.BlockSpec((TM, D_MODEL), lambda i: (i, 0)),
                  _const_spec((1, D_MODEL)),
                  _layer_spec((D_MODEL, D_FF), idx), _layer_spec((D_MODEL, D_FF), idx),
                  _layer_spec((D_FF, D_MODEL), idx)],
        out_specs=pl.BlockSpec((TM, D_MODEL), lambda i: (i, 0)),
        out_shape=jax.ShapeDtypeStruct((n_rows, D_MODEL), _F32),
        scratch_shapes=[pltpu.VMEM((TM, D_MODEL), _BF16)],
        compiler_params=pltpu.CompilerParams(
            dimension_semantics=("arbitrary",), vmem_limit_bytes=VMEM_LIMIT),
        name="ffn",
    )(x, g, w1, w3, w2)


def _mixer_c_kernel(tiles_per_seq, x_ref, xp_ref, xq_ref, g_ref, w_in_f32, cw_ref, w_out_f32,
                    g2_ref, rhi_ref, rlo_ref, o_ref, route_ref, u_scr, w_in_ref, w_out_ref):
    i = pl.program_id(0)
    first = (i % tiles_per_seq) == 0
    last = (i % tiles_per_seq) == tiles_per_seq - 1

    @pl.when(i == 0)
    def _():
        w_in_ref[...] = w_in_f32[0].astype(_BF16)
        w_out_ref[...] = w_out_f32[0].astype(_BF16)

    x = x_ref[...]
    g = g_ref[...]
    xn = _rms(x, g).astype(_BF16)
    z = _dot(xn, w_in_ref[...])

    xh = jnp.concatenate([xp_ref[...], xq_ref[...]], axis=0)
    xhn = _rms(xh, g).astype(_BF16)
    zh = _dot(xhn, w_in_ref[:, D_MODEL:])
    uh = zh[:, :D_MODEL] * zh[:, D_MODEL:]
    u_scr[0:HALO_C, :] = jnp.where(first, 0.0, uh[:HALO_C])
    u_scr[HALO_C + TM:, :] = jnp.where(last, 0.0, uh[HALO_C:])
    u_scr[HALO_C:HALO_C + TM, :] = z[:, D_MODEL:2 * D_MODEL] * z[:, 2 * D_MODEL:]

    conv = (u_scr[HALO_C - 1:HALO_C - 1 + TM, :] * cw_ref[0:1, :]
            + u_scr[HALO_C:HALO_C + TM, :] * cw_ref[1:2, :]
            + u_scr[HALO_C + 1:HALO_C + 1 + TM, :] * cw_ref[2:3, :])
    y = (z[:, :D_MODEL] * conv).astype(_BF16)
    x1 = x + _dot(y, w_out_ref[...])
    o_ref[...] = x1.reshape(TM, *ROW_TILE)

    hn = _rms(x1, g2_ref[...])
    hi = hn.astype(_BF16)
    lo = (hn - hi.astype(_F32)).astype(_BF16)
    rhi = rhi_ref[...]
    logits = _dot(hi, rhi) + (_dot(lo, rhi) + _dot(hi, rlo_ref[...]))

    lane = lax.broadcasted_iota(jnp.int32, logits.shape, 1).astype(_F32)
    neg = jnp.float32(-jnp.inf)
    lg = jnp.where(lane < N_EXPERTS, logits, neg)
    m1 = jnp.max(lg, axis=-1, keepdims=True)
    i1 = jnp.min(jnp.where(lg == m1, lane, float(LANES)), axis=-1, keepdims=True)
    lg2 = jnp.where(lane == i1, neg, lg)
    m2 = jnp.max(lg2, axis=-1, keepdims=True)
    i2 = jnp.min(jnp.where(lg2 == m2, lane, float(LANES)), axis=-1, keepdims=True)
    e2 = jnp.exp(m2 - m1)
    den = 1.0 + e2
    route_ref[...] = jnp.where(lane == 0, i1, jnp.where(lane == 1, i2, jnp.where(
        lane == GATE_LANE, 1.0 / den, jnp.where(lane == GATE_LANE + 1, e2 / den, 0.0))))


def _mixer_c(x, seq, idx, p):
    n_rows = x.shape[0]
    in_specs = _tile_specs([(ROW, 0)], n_rows, HALO_C) + [
        _const_spec((1, D_MODEL)),
        _layer_spec((D_MODEL, 3 * D_MODEL), idx),
        _const_spec((CONV_C_WIDTH, D_MODEL)),
        _layer_spec((D_MODEL, D_MODEL), idx),
        _const_spec((1, D_MODEL)),
        _const_spec((D_MODEL, LANES)), _const_spec((D_MODEL, LANES)),
    ]
    return pl.pallas_call(
        functools.partial(_mixer_c_kernel, seq // TM),
        grid=(n_rows // TM,),
        in_specs=in_specs,
        out_specs=[pl.BlockSpec((TM, *ROW_TILE), lambda i: (i, 0, 0)),
                   pl.BlockSpec((TM, LANES), lambda i: (i, 0))],
        out_shape=[jax.ShapeDtypeStruct((n_rows, *ROW_TILE), _F32),
                   jax.ShapeDtypeStruct((n_rows, LANES), _F32)],
        scratch_shapes=[pltpu.VMEM((TM + 2 * HALO_C, D_MODEL), _F32),
                        pltpu.VMEM((D_MODEL, 3 * D_MODEL), _BF16),
                        pltpu.VMEM((D_MODEL, D_MODEL), _BF16)],
        compiler_params=pltpu.CompilerParams(
            dimension_semantics=("arbitrary",), vmem_limit_bytes=VMEM_LIMIT),
        name="mixer_c",
    )(x, x, x, p["g"], p["w_in"], p["conv_w"], p["w_out"], p["g_ffn"], p["r_hi"], p["r_lo"])


def _plan_kernel(te_ref, r0_ref, last_ref, cum_ref, u_ref):
    del te_ref
    i = pl.program_id(0)
    n_blocks = last_ref.shape[-1]
    rank = (r0_ref[i] + lax.broadcasted_iota(jnp.int32, (TM, 1), 0)).astype(_F32)
    blk = jnp.sum(jnp.where(last_ref[0] <= rank, 1.0, 0.0), axis=1, keepdims=True)
    block_ids = lax.broadcasted_iota(jnp.int32, (TM, n_blocks), 1).astype(_F32)
    onehot = jnp.where(block_ids == blk, 1.0, 0.0).astype(_BF16)
    digits = _dot(onehot, cum_ref[0])
    cum = digits[:, :LANES] * 256.0 + digits[:, LANES:]
    within = jnp.sum(jnp.where(cum <= rank, 1.0, 0.0), axis=1, keepdims=True)
    u = jnp.minimum(blk * LANES + within, n_blocks * LANES - 1.0)
    u_ref[0] = jnp.broadcast_to(u, (TM, LANES)).T[0:1, :].astype(jnp.int32)


def _route_plan(route, n_rows):
    n_pairs = TOP_K * n_rows
    n_blocks = n_pairs // LANES
    n_tiles = (n_pairs + N_EXPERTS * (TM - 1)) // TM
    e_cat = jnp.concatenate([route[:, k] for k in range(TOP_K)]).astype(jnp.int32)
    onehot = (e_cat[None, :] == jnp.arange(N_EXPERTS, dtype=jnp.int32)[:, None]).astype(jnp.int32)
    cum = jnp.cumsum(onehot, axis=1)
    counts = cum[:, -1]
    tiles_e = (counts + TM - 1) // TM
    tile_end = jnp.cumsum(tiles_e)
    n_used = tile_end[-1]
    tile_ids = jnp.arange(n_tiles, dtype=jnp.int32)
    tile_expert = jnp.sum(
        (tile_end[None, :] <= jnp.minimum(tile_ids, n_used - 1)[:, None]).astype(jnp.int32), axis=1)
    r0 = (tile_ids - (tile_end - tiles_e)[tile_expert]) * TM
    n_valid = jnp.where(tile_ids < n_used, jnp.clip(counts[tile_expert] - r0, 0, TM), 0)
    cum_blocks = cum.reshape(N_EXPERTS, n_blocks, LANES)
    block_last = cum_blocks[:, :, -1].astype(_F32).reshape(N_EXPERTS, 1, n_blocks)
    cum_digits = jnp.concatenate([cum_blocks // 256, cum_blocks % 256], axis=-1).astype(_BF16)

    u = pl.pallas_call(
        _plan_kernel,
        grid_spec=pltpu.PrefetchScalarGridSpec(
            num_scalar_prefetch=2,
            grid=(n_tiles,),
            in_specs=[pl.BlockSpec((1, 1, n_blocks), lambda i, te, *_: (te[i], 0, 0)),
                      pl.BlockSpec((1, n_blocks, 2 * LANES), lambda i, te, *_: (te[i], 0, 0))],
            out_specs=pl.BlockSpec((1, 1, TM), lambda i, *_: (i, 0, 0)),
        ),
        out_shape=jax.ShapeDtypeStruct((n_tiles, 1, TM), jnp.int32),
        compiler_params=pltpu.CompilerParams(
            dimension_semantics=("arbitrary",), vmem_limit_bytes=VMEM_LIMIT),
        name="plan",
    )(tile_expert, r0, block_last, cum_digits)
    src = jnp.where(u >= n_rows, u - n_rows, u)
    row = jnp.arange(TM, dtype=jnp.int32)[None, None, :]
    spare = n_pairs + (tile_ids % 2)[:, None, None] * TM + row
    dst = jnp.where(row < n_valid[:, None, None], u, spare)
    dst = jnp.concatenate([n_pairs + TM + row, dst], axis=0)
    return tile_expert, n_used.reshape(1), src, dst


def _moe_kernel(te_ref, nu_ref, src_ref, srcn_ref, dstp_ref, dstc_ref, x_hbm, g_ref,
                w1_ref, w3_ref, w2_ref, ys_hbm, xbuf, ybuf, xn_ref, gsem, ssem, zsem):
    del te_ref
    i = pl.program_id(0)
    last = pl.num_programs(0) - 1
    n_used = nu_ref[0]
    slot = i % 2
    other = 1 - slot
    n_pairs = ys_hbm.shape[0] - 2 * TM

    def gather_row(idx_ref, s, r):
        return pltpu.make_async_copy(x_hbm.at[idx_ref[0, 0, r]], xbuf.at[s, r], gsem.at[s])

    def scatter_row(idx_ref, s, r):
        return pltpu.make_async_copy(ybuf.at[s, r], ys_hbm.at[idx_ref[0, 0, r]], ssem.at[s])

    def rolled(row_copy):
        def body(r, carry):
            row_copy(r).start()
            return carry
        lax.fori_loop(0, TM, body, 0)

    def wait_gather(s):
        pltpu.make_async_copy(x_hbm.at[pl.ds(0, TM)], xbuf.at[s], gsem.at[s]).wait()

    def wait_scatter(s):
        pltpu.make_async_copy(ybuf.at[s], ys_hbm.at[pl.ds(0, TM)], ssem.at[s]).wait()

    @pl.when(i == 0)
    def _():
        rolled(lambda r: gather_row(src_ref, 0, r))
        ybuf[1] = jnp.zeros((TM, *ROW_TILE), _F32)
        fill = pltpu.make_async_copy(ybuf.at[1], ys_hbm.at[pl.ds(n_pairs, TM)], zsem)
        fill.start()
        fill.wait()

    wait_gather(slot)

    @pl.when(i < n_used)
    def _():
        xn_ref[...] = _rms(xbuf[slot].reshape(TM, D_MODEL), g_ref[...]).astype(_BF16)

        n_pieces = _swiglu_pieces()

        def neighbour_rows(p):
            for r in range(p * TM // n_pieces, (p + 1) * TM // n_pieces):
                gather_row(srcn_ref, other, r).start(priority=r % 2)
                scatter_row(dstp_ref, other, r).start(priority=r % 2)

        y = _swiglu(xn_ref, w1_ref.at[0, 0], w3_ref.at[0, 0], w2_ref.at[0, 0], side_work=neighbour_rows)

        @pl.when(i >= 1)
        def _():
            wait_scatter(slot)

        ybuf[slot] = jnp.concatenate(y, axis=1).reshape(TM, *ROW_TILE)

    @pl.when(i >= n_used)
    def _():
        @pl.when(i < last)
        def _():
            rolled(lambda r: gather_row(srcn_ref, other, r))

        @pl.when(i == n_used)
        def _():
            rolled(lambda r: scatter_row(dstp_ref, other, r))

    @pl.when(i == last)
    def _():
        @pl.when(i < n_used)
        def _():
            wait_gather(other)
            rolled(lambda r: scatter_row(dstc_ref, slot, r))

        wait_scatter(0)
        wait_scatter(1)


def _moe(x, g, idx, w1, w3, w2, tile_expert, n_used, src, dst):
    n_tiles = src.shape[0]
    smem_rows = functools.partial(pl.BlockSpec, (1, 1, TM), memory_space=pltpu.SMEM)

    def expert_spec(shape):
        return pl.BlockSpec((1, 1, *shape), lambda i, te, *_: (idx, te[i], 0, 0),
                            pipeline_mode=pl.Buffered(1))

    grid_spec = pltpu.PrefetchScalarGridSpec(
        num_scalar_prefetch=2,
        grid=(n_tiles,),
        in_specs=[
            smem_rows(lambda i, *_: (i, 0, 0)),
            smem_rows(lambda i, *_: (jnp.minimum(i + 1, n_tiles - 1), 0, 0)),
            smem_rows(lambda i, *_: (i, 0, 0)),
            smem_rows(lambda i, *_: (i + 1, 0, 0)),
            pl.BlockSpec(memory_space=pl.ANY),
            pl.BlockSpec((1, D_MODEL), lambda i, *_: (0, 0)),
            expert_spec((D_MODEL, D_FF)), expert_spec((D_MODEL, D_FF)), expert_spec((D_FF, D_MODEL)),
        ],
        out_specs=pl.BlockSpec(memory_space=pl.ANY),
        scratch_shapes=[pltpu.VMEM((2, TM, *ROW_TILE), _F32),
                        pltpu.VMEM((2, TM, *ROW_TILE), _F32),
                        pltpu.VMEM((TM, D_MODEL), _BF16),
                        pltpu.SemaphoreType.DMA((2,)),
                        pltpu.SemaphoreType.DMA((2,)),
                        pltpu.SemaphoreType.DMA(())],
    )
    return pl.pallas_call(
        _moe_kernel,
        grid_spec=grid_spec,
        out_shape=jax.ShapeDtypeStruct((TOP_K * x.shape[0] + 2 * TM, *ROW_TILE), _F32),
        compiler_params=pltpu.CompilerParams(
            dimension_semantics=("arbitrary",), vmem_limit_bytes=VMEM_LIMIT),
        name="moe",
    )(tile_expert, n_used, src, src, dst, dst, x, g, w1, w3, w2)


def _final_kernel(*refs):
    *in_refs, g_ref, o_ref = refs
    o_ref[...] = _rms(_residual_in(in_refs), g_ref[...])


def _final_norm(h, moe_out, g):
    n_rows = h.shape[0]
    arrays, descr = _layer_inputs(h, moe_out, n_rows)
    return pl.pallas_call(
        _final_kernel,
        grid=(n_rows // TM,),
        in_specs=_tile_specs(descr, n_rows, TM)[:len(descr)] + [_const_spec((1, D_MODEL))],
        out_specs=pl.BlockSpec((TM, D_MODEL), lambda i: (i, 0)),
        out_shape=jax.ShapeDtypeStruct((n_rows, D_MODEL), _F32),
        compiler_params=pltpu.CompilerParams(
            dimension_semantics=("arbitrary",), vmem_limit_bytes=VMEM_LIMIT),
        name="final_norm",
    )(*arrays, g)


def _row(v):
    return v.reshape(1, -1)


def kernel(x, norm_mix_g, norm_ffn_g, ev_w_in, ev_ws, ev_bs, ev_ln_a_g, ev_ln_a_b, ev_conv_w,
           ev_conv_b, ev_ln_b_g, ev_ln_b_b, ev_w_out, od_w_in, od_conv_w, od_w_out, ffn_w1,
           ffn_w3, ffn_w2, moe_router, moe_w1, moe_w3, moe_w2, final_g):
    bsz, seq, d = x.shape
    n_rows = bsz * seq
    depth = norm_mix_g.shape[0]
    assert d == D_MODEL and seq % TM == 0 and depth % 2 == 0 and TOP_K == 2

    h = x.reshape(n_rows, d)
    moe_out = None
    for layer in range(depth):
        i = layer // 2
        if layer % 2 == 0:
            p = dict(
                g=_row(norm_mix_g[layer]), w_in=ev_w_in, ws=ev_ws,
                bsb=jnp.broadcast_to(ev_bs[i][:, :, None], (A_HEADS, CHUNK, LANES)),
                ln_a_g=_row(ev_ln_a_g[i]), ln_a_b=_row(ev_ln_a_b[i]), conv_w=ev_conv_w[i],
                conv_b=_row(ev_conv_b[i]), ln_b_g=_row(ev_ln_b_g[i]), ln_b_b=_row(ev_ln_b_b[i]),
                w_out=ev_w_out)
            h = _mixer_ab(h, moe_out, n_rows, seq, i, p)
            h = _ffn(h, _row(norm_ffn_g[layer]), i, ffn_w1, ffn_w3, ffn_w2)
            moe_out = None
        else:
            router = jnp.pad(moe_router[i], ((0, 0), (0, LANES - N_EXPERTS)))
            r_hi = router.astype(_BF16)
            r_lo = (router - r_hi.astype(_F32)).astype(_BF16)
            p = dict(g=_row(norm_mix_g[layer]), w_in=od_w_in, conv_w=od_conv_w[i], w_out=od_w_out,
                     g_ffn=_row(norm_ffn_g[layer]), r_hi=r_hi, r_lo=r_lo)
            h, route = _mixer_c(h, seq, i, p)
            ys = _moe(h, _row(norm_ffn_g[layer]), i, moe_w1, moe_w3, moe_w2, *_route_plan(route, n_rows))
            moe_out = (ys, route)
    out = _final_norm(h, moe_out, _row(final_g))
    return out.reshape(bsz, seq, d)
```

```python
import functools

import jax
import jax.numpy as jnp
from jax import lax
from jax.experimental import pallas as pl
from jax.experimental.pallas import tpu as pltpu

D_MODEL = 1024
A_WIDTH = 512
A_HEADS = 4
CHUNK = 128
B_WIDTH = 512
CONV_B_WIDTH = 31
CONV_B_PAD = 15
CONV_C_WIDTH = 3
D_FF = 2816
N_EXPERTS = 8
TOP_K = 2
RMS_EPS = 1e-6
LN_EPS = 1e-5

LANES = 128
SUBLANES = 8
MXU_N = 256
D_BLOCKS = D_MODEL // LANES
assert D_BLOCKS == SUBLANES
TM = 512
HALO_B = 16
HALO_C = 8
FF_SPLITS = (0, 512, 1024, 1536, 2048, 2560, D_FF)
CONV_ROWS = 64
GATHER_SLOTS = 3
MIX_ROWS = 256
VMEM_LIMIT = 60 * 1024 * 1024
GATE_LANE = TOP_K

_F32 = jnp.float32
_BF16 = jnp.bfloat16


def _rms(x, g):
    y = x * lax.rsqrt(jnp.mean(x * x, axis=-1, keepdims=True) + RMS_EPS)
    return y * g


def _layernorm(x, g, b):
    mu = jnp.mean(x, axis=-1, keepdims=True)
    xc = x - mu
    y = xc * lax.rsqrt(jnp.mean(xc * xc, axis=-1, keepdims=True) + LN_EPS)
    return y * g + b


def _sigmoid(x):
    return 1.0 / (1.0 + jnp.exp(-x))


def _gelu_tanh(x):
    c = 0.7978845608028654
    return 0.5 * x * (1.0 + jnp.tanh(c * (x + 0.044715 * (x * x * x))))


def _dot(a, b):
    return jnp.dot(a, b, preferred_element_type=_F32)


def _residual_in(refs, rows=slice(None)):
    if len(refs) == 1:
        return refs[0][rows, :]
    x, ya, yb = (ref[rows].reshape(-1, D_MODEL) for ref in refs[:3])
    r = refs[3][rows, :]
    return x + r[:, GATE_LANE:GATE_LANE + 1] * ya + r[:, GATE_LANE + 1:GATE_LANE + 2] * yb


def _mixer_ab_kernel(n_in, tiles_per_seq, *refs):
    mains = refs[0:n_in]
    prevs = refs[n_in:2 * n_in]
    nexts = refs[2 * n_in:3 * n_in]
    (g_ref, w_in_f32, ws_f32, bsb_ref, lnag_ref, lnab_ref, cw_ref, cb_ref,
     lnbg_ref, lnbb_ref, w_out_f32, o_ref, glu_scr, shift_scr, mix_scr,
     w_in_ref, ws_ref, w_out_ref, x_scr) = refs[3 * n_in:]

    i = pl.program_id(0)
    first = (i % tiles_per_seq) == 0
    last = (i % tiles_per_seq) == tiles_per_seq - 1

    @pl.when(i == 0)
    def _():
        w_in_ref[...] = w_in_f32[0].astype(_BF16)
        ws_ref[...] = ws_f32[0].astype(_BF16)
        w_out_ref[...] = w_out_f32[0].astype(_BF16)

    g = g_ref[...]
    xh = jnp.concatenate([_residual_in(prevs), _residual_in(nexts)], axis=0)
    xhn = _rms(xh, g).astype(_BF16)
    zh = _dot(xhn, w_in_ref[:, 2 * A_WIDTH:])
    glu_h = zh[:, :B_WIDTH] * _sigmoid(zh[:, B_WIDTH:])
    glu_scr[0:HALO_B, :] = jnp.where(first, 0.0, glu_h[:HALO_B])
    glu_scr[HALO_B + TM:, :] = jnp.where(last, 0.0, glu_h[HALO_B:])

    blocks = [slice(b * MIX_ROWS, (b + 1) * MIX_ROWS) for b in range(TM // MIX_ROWS)]
    for blk in blocks:
        x = _residual_in(mains, blk)
        x_scr[blk, :] = x
        xn = _rms(x, g).astype(_BF16)
        z = _dot(xn, w_in_ref[...])
        glu_scr[HALO_B + blk.start:HALO_B + blk.stop, :] = (
            z[:, 2 * A_WIDTH:2 * A_WIDTH + B_WIDTH] * _sigmoid(z[:, 2 * A_WIDTH + B_WIDTH:]))

        a_u = _gelu_tanh(z[:, :A_WIDTH])
        a_v = _layernorm(_gelu_tanh(z[:, A_WIDTH:2 * A_WIDTH]), lnag_ref[...], lnab_ref[...])
        a_vb = a_v.astype(_BF16)
        for c in range(MIX_ROWS // CHUNK):
            rows = slice(c * CHUNK, (c + 1) * CHUNK)
            for h in range(A_HEADS):
                cols = slice(h * LANES, (h + 1) * LANES)
                v = _dot(ws_ref[h], a_vb[rows, cols]) + bsb_ref[h]
                mix_scr[blk.start + c * CHUNK:blk.start + (c + 1) * CHUNK, cols] = (
                    a_u[rows, cols] * v).astype(_BF16)

    n_shift_rows = TM + 2 * HALO_B - SUBLANES
    for s in range(1, SUBLANES):
        shift_scr[s - 1] = glu_scr[s:s + n_shift_rows, :]
    cb = cb_ref[...]
    for r in range(TM // CONV_ROWS):
        base = r * CONV_ROWS
        acc = jnp.zeros((CONV_ROWS, B_WIDTH), _F32)
        for k in range(CONV_B_WIDTH):
            off = base + HALO_B - CONV_B_PAD + k
            s = off % SUBLANES
            src = glu_scr if s == 0 else shift_scr.at[s - 1]
            acc = acc + src[off - s:off - s + CONV_ROWS, :] * cw_ref[k:k + 1, :]
        y = _layernorm(acc + cb, lnbg_ref[...], lnbb_ref[...])
        mix_scr[base:base + CONV_ROWS, A_WIDTH:] = (y * _sigmoid(y)).astype(_BF16)

    for blk in blocks:
        o_ref[blk, :] = x_scr[blk, :] + _dot(mix_scr[blk, :], w_out_ref[...])


def _const_spec(shape):
    nd = len(shape)
    return pl.BlockSpec(shape, lambda *_: (0,) * nd, pipeline_mode=pl.Buffered(1))


def _layer_spec(shape, idx):
    nd = len(shape)
    return pl.BlockSpec((1, *shape), lambda *_: (idx,) + (0,) * nd, pipeline_mode=pl.Buffered(1))


ROW = (D_MODEL,)
ROW_TILE = (D_BLOCKS, LANES)


def _tile_specs(rows_and_offsets, n_rows, halo):
    nb = n_rows // halo
    per = TM // halo
    mains, prevs, nexts = [], [], []
    for row, off in rows_and_offsets:
        zeros = (0,) * len(row)
        mains.append(pl.BlockSpec((TM, *row), lambda i, o=off, z=zeros: (i + o // TM, *z)))
        prevs.append(pl.BlockSpec(
            (halo, *row), lambda i, o=off, z=zeros: (jnp.maximum(i * per - 1, 0) + o // halo, *z)))
        nexts.append(pl.BlockSpec(
            (halo, *row),
            lambda i, o=off, z=zeros: (jnp.minimum((i + 1) * per, nb - 1) + o // halo, *z)))
    return mains + prevs + nexts


def _layer_inputs(h, moe_out, n_rows):
    if moe_out is None:
        return [h], [(ROW, 0)]
    ys, route = moe_out
    return [h, ys, ys, route], [(ROW_TILE, 0), (ROW_TILE, 0), (ROW_TILE, n_rows), ((LANES,), 0)]


def _mixer_ab(h, moe_out, n_rows, seq, idx, p):
    arrays, descr = _layer_inputs(h, moe_out, n_rows)
    in_specs = _tile_specs(descr, n_rows, HALO_B) + [
        _const_spec((1, D_MODEL)),
        _layer_spec((D_MODEL, 2 * A_WIDTH + 2 * B_WIDTH), idx),
        _layer_spec((A_HEADS, CHUNK, CHUNK), idx),
        _const_spec((A_HEADS, CHUNK, LANES)),
        _const_spec((1, A_WIDTH)), _const_spec((1, A_WIDTH)),
        _const_spec((CONV_B_WIDTH, B_WIDTH)), _const_spec((1, B_WIDTH)),
        _const_spec((1, B_WIDTH)), _const_spec((1, B_WIDTH)),
        _layer_spec((A_WIDTH + B_WIDTH, D_MODEL), idx),
    ]
    return pl.pallas_call(
        functools.partial(_mixer_ab_kernel, len(arrays), seq // TM),
        grid=(n_rows // TM,),
        in_specs=in_specs,
        out_specs=pl.BlockSpec((TM, D_MODEL), lambda i: (i, 0)),
        out_shape=jax.ShapeDtypeStruct((n_rows, D_MODEL), _F32),
        scratch_shapes=[pltpu.VMEM((TM + 2 * HALO_B, B_WIDTH), _F32),
                        pltpu.VMEM((SUBLANES - 1, TM + 2 * HALO_B - SUBLANES, B_WIDTH), _F32),
                        pltpu.VMEM((TM, A_WIDTH + B_WIDTH), _BF16),
                        pltpu.VMEM((D_MODEL, 2 * A_WIDTH + 2 * B_WIDTH), _BF16),
                        pltpu.VMEM((A_HEADS, CHUNK, CHUNK), _BF16),
                        pltpu.VMEM((A_WIDTH + B_WIDTH, D_MODEL), _BF16),
                        pltpu.VMEM((TM, D_MODEL), _F32)],
        compiler_params=pltpu.CompilerParams(
            dimension_semantics=("arbitrary",), vmem_limit_bytes=VMEM_LIMIT),
        name="mixer_ab",
    )(*(arrays * 3), p["g"], p["w_in"], p["ws"], p["bsb"], p["ln_a_g"], p["ln_a_b"],
      p["conv_w"], p["conv_b"], p["ln_b_g"], p["ln_b_b"], p["w_out"])


def _swiglu_pieces():
    n = 0
    for lo, hi in zip(FF_SPLITS[:-1], FF_SPLITS[1:]):
        n += 2 * ((hi - lo) // MXU_N) + D_MODEL // MXU_N
    return n


def _swiglu(xn_ref, w1_ref, w3_ref, w2_ref, side_work=None):
    piece = [0]

    def dot_cols(a, w_ref, rows, lo, hi):
        outs = []
        for c in range(lo, hi, MXU_N):
            if side_work is not None:
                side_work(piece[0])
            piece[0] += 1
            outs.append(_dot(a(), w_ref[rows, c:c + MXU_N].astype(_BF16)))
        return outs

    acc = None
    for lo, hi in zip(FF_SPLITS[:-1], FF_SPLITS[1:]):
        h1 = jnp.concatenate(dot_cols(lambda: xn_ref[...], w1_ref, slice(None), lo, hi), axis=1)
        h3 = jnp.concatenate(dot_cols(lambda: xn_ref[...], w3_ref, slice(None), lo, hi), axis=1)
        a = (h1 * _sigmoid(h1) * h3).astype(_BF16)
        d = dot_cols(lambda: a, w2_ref, slice(lo, hi), 0, D_MODEL)
        acc = d if acc is None else [p + q for p, q in zip(acc, d)]
    return acc


def _ffn_kernel(x_ref, g_ref, w1_ref, w3_ref, w2_ref, o_ref, xn_ref):
    x = x_ref[...]
    xn_ref[...] = _rms(x, g_ref[...]).astype(_BF16)
    y = _swiglu(xn_ref, w1_ref.at[0], w3_ref.at[0], w2_ref.at[0])
    o_ref[...] = x + jnp.concatenate(y, axis=1)


def _ffn(x, g, idx, w1, w3, w2):
    n_rows = x.shape[0]
    return pl.pallas_call(
        _ffn_kernel,
        grid=(n_rows // TM,),
        in_specs=[pl.BlockSpec((TM, D_MODEL), lambda i: (i, 0)),
                  _const_spec((1, D_MODEL)),
                  _layer_spec((D_MODEL, D_FF), idx), _layer_spec((D_MODEL, D_FF), idx),
                  _layer_spec((D_FF, D_MODEL), idx)],
        out_specs=pl.BlockSpec((TM, D_MODEL), lambda i: (i, 0)),
        out_shape=jax.ShapeDtypeStruct((n_rows, D_MODEL), _F32),
        scratch_shapes=[pltpu.VMEM((TM, D_MODEL), _BF16)],
        compiler_params=pltpu.CompilerParams(
            dimension_semantics=("arbitrary",), vmem_limit_bytes=VMEM_LIMIT),
        name="ffn",
    )(x, g, w1, w3, w2)


def _mixer_c_kernel(tiles_per_seq, x_ref, xp_ref, xq_ref, g_ref, w_in_f32, cw_ref, w_out_f32,
                    g2_ref, rhi_ref, rlo_ref, o_ref, route_ref, u_scr, w_in_ref, w_out_ref):
    i = pl.program_id(0)
    first = (i % tiles_per_seq) == 0
    last = (i % tiles_per_seq) == tiles_per_seq - 1

    @pl.when(i == 0)
    def _():
        w_in_ref[...] = w_in_f32[0].astype(_BF16)
        w_out_ref[...] = w_out_f32[0].astype(_BF16)

    g = g_ref[...]
    xh = jnp.concatenate([xp_ref[...], xq_ref[...]], axis=0)
    xhn = _rms(xh, g).astype(_BF16)
    zh = _dot(xhn, w_in_ref[:, D_MODEL:])
    uh = zh[:, :D_MODEL] * zh[:, D_MODEL:]
    u_scr[0:HALO_C, :] = jnp.where(first, 0.0, uh[:HALO_C])
    u_scr[HALO_C + TM:, :] = jnp.where(last, 0.0, uh[HALO_C:])

    blocks = [slice(b * MIX_ROWS, (b + 1) * MIX_ROWS) for b in range(TM // MIX_ROWS)]
    gate_b = []
    for rows in blocks:
        xn = _rms(x_ref[rows, :], g).astype(_BF16)
        z = _dot(xn, w_in_ref[...])
        u_scr[HALO_C + rows.start:HALO_C + rows.stop, :] = z[:, D_MODEL:2 * D_MODEL] * z[:, 2 * D_MODEL:]
        gate_b.append(z[:, :D_MODEL])

    for rows, gb in zip(blocks, gate_b):
        lo_row = HALO_C + rows.start
        conv = (u_scr[lo_row - 1:lo_row - 1 + MIX_ROWS, :] * cw_ref[0:1, :]
                + u_scr[lo_row:lo_row + MIX_ROWS, :] * cw_ref[1:2, :]
                + u_scr[lo_row + 1:lo_row + 1 + MIX_ROWS, :] * cw_ref[2:3, :])
        y = (gb * conv).astype(_BF16)
        x1 = x_ref[rows, :] + _dot(y, w_out_ref[...])
        o_ref[rows] = x1.reshape(MIX_ROWS, *ROW_TILE)

        hn = _rms(x1, g2_ref[...])
        hi = hn.astype(_BF16)
        lo = (hn - hi.astype(_F32)).astype(_BF16)
        rhi = rhi_ref[...]
        logits = _dot(hi, rhi) + (_dot(lo, rhi) + _dot(hi, rlo_ref[...]))

        lane = lax.broadcasted_iota(jnp.int32, logits.shape, 1).astype(_F32)
        neg = jnp.float32(-jnp.inf)
        lg = jnp.where(lane < N_EXPERTS, logits, neg)
        m1 = jnp.max(lg, axis=-1, keepdims=True)
        i1 = jnp.min(jnp.where(lg == m1, lane, float(LANES)), axis=-1, keepdims=True)
        lg2 = jnp.where(lane == i1, neg, lg)
        m2 = jnp.max(lg2, axis=-1, keepdims=True)
        i2 = jnp.min(jnp.where(lg2 == m2, lane, float(LANES)), axis=-1, keepdims=True)
        e2 = jnp.exp(m2 - m1)
        den = 1.0 + e2
        route_ref[rows, :] = jnp.where(lane == 0, i1, jnp.where(lane == 1, i2, jnp.where(
            lane == GATE_LANE, 1.0 / den, jnp.where(lane == GATE_LANE + 1, e2 / den, 0.0))))


def _mixer_c(x, seq, idx, p):
    n_rows = x.shape[0]
    in_specs = _tile_specs([(ROW, 0)], n_rows, HALO_C) + [
        _const_spec((1, D_MODEL)),
        _layer_spec((D_MODEL, 3 * D_MODEL), idx),
        _const_spec((CONV_C_WIDTH, D_MODEL)),
        _layer_spec((D_MODEL, D_MODEL), idx),
        _const_spec((1, D_MODEL)),
        _const_spec((D_MODEL, LANES)), _const_spec((D_MODEL, LANES)),
    ]
    return pl.pallas_call(
        functools.partial(_mixer_c_kernel, seq // TM),
        grid=(n_rows // TM,),
        in_specs=in_specs,
        out_specs=[pl.BlockSpec((TM, *ROW_TILE), lambda i: (i, 0, 0)),
                   pl.BlockSpec((TM, LANES), lambda i: (i, 0))],
        out_shape=[jax.ShapeDtypeStruct((n_rows, *ROW_TILE), _F32),
                   jax.ShapeDtypeStruct((n_rows, LANES), _F32)],
        scratch_shapes=[pltpu.VMEM((TM + 2 * HALO_C, D_MODEL), _F32),
                        pltpu.VMEM((D_MODEL, 3 * D_MODEL), _BF16),
                        pltpu.VMEM((D_MODEL, D_MODEL), _BF16)],
        compiler_params=pltpu.CompilerParams(
            dimension_semantics=("arbitrary",), vmem_limit_bytes=VMEM_LIMIT),
        name="mixer_c",
    )(x, x, x, p["g"], p["w_in"], p["conv_w"], p["w_out"], p["g_ffn"], p["r_hi"], p["r_lo"])


def _plan_kernel(te_ref, r0_ref, last_ref, cum_ref, u_ref):
    del te_ref
    i = pl.program_id(0)
    n_blocks = last_ref.shape[-1]
    rank = (r0_ref[i] + lax.broadcasted_iota(jnp.int32, (TM, 1), 0)).astype(_F32)
    blk = jnp.sum(jnp.where(last_ref[0] <= rank, 1.0, 0.0), axis=1, keepdims=True)
    block_ids = lax.broadcasted_iota(jnp.int32, (TM, n_blocks), 1).astype(_F32)
    onehot = jnp.where(block_ids == blk, 1.0, 0.0).astype(_BF16)
    digits = _dot(onehot, cum_ref[0])
    cum = digits[:, :LANES] * 256.0 + digits[:, LANES:]
    within = jnp.sum(jnp.where(cum <= rank, 1.0, 0.0), axis=1, keepdims=True)
    u = jnp.minimum(blk * LANES + within, n_blocks * LANES - 1.0)
    u_ref[0] = jnp.broadcast_to(u, (TM, LANES)).T[0:1, :].astype(jnp.int32)


def _route_plan(route, n_rows):
    n_pairs = TOP_K * n_rows
    n_blocks = n_pairs // LANES
    n_tiles = (n_pairs + N_EXPERTS * (TM - 1)) // TM
    e_cat = jnp.concatenate([route[:, k] for k in range(TOP_K)]).astype(jnp.int32)
    onehot = (e_cat[None, :] == jnp.arange(N_EXPERTS, dtype=jnp.int32)[:, None]).astype(jnp.int32)
    cum = jnp.cumsum(onehot, axis=1)
    counts = cum[:, -1]
    tiles_e = (counts + TM - 1) // TM
    tile_end = jnp.cumsum(tiles_e)
    n_used = tile_end[-1]
    tile_ids = jnp.arange(n_tiles, dtype=jnp.int32)
    tile_expert = jnp.sum(
        (tile_end[None, :] <= jnp.minimum(tile_ids, n_used - 1)[:, None]).astype(jnp.int32), axis=1)
    r0 = (tile_ids - (tile_end - tiles_e)[tile_expert]) * TM
    n_valid = jnp.where(tile_ids < n_used, jnp.clip(counts[tile_expert] - r0, 0, TM), 0)
    cum_blocks = cum.reshape(N_EXPERTS, n_blocks, LANES)
    block_last = cum_blocks[:, :, -1].astype(_F32).reshape(N_EXPERTS, 1, n_blocks)
    cum_digits = jnp.concatenate([cum_blocks // 256, cum_blocks % 256], axis=-1).astype(_BF16)

    u = pl.pallas_call(
        _plan_kernel,
        grid_spec=pltpu.PrefetchScalarGridSpec(
            num_scalar_prefetch=2,
            grid=(n_tiles,),
            in_specs=[pl.BlockSpec((1, 1, n_blocks), lambda i, te, *_: (te[i], 0, 0)),
                      pl.BlockSpec((1, n_blocks, 2 * LANES), lambda i, te, *_: (te[i], 0, 0))],
            out_specs=pl.BlockSpec((1, 1, TM), lambda i, *_: (i, 0, 0)),
        ),
        out_shape=jax.ShapeDtypeStruct((n_tiles, 1, TM), jnp.int32),
        compiler_params=pltpu.CompilerParams(
            dimension_semantics=("arbitrary",), vmem_limit_bytes=VMEM_LIMIT),
        name="plan",
    )(tile_expert, r0, block_last, cum_digits)
    src = jnp.where(u >= n_rows, u - n_rows, u)
    row = jnp.arange(TM, dtype=jnp.int32)[None, None, :]
    spare = n_pairs + (tile_ids % 2)[:, None, None] * TM + row
    dst = jnp.where(row < n_valid[:, None, None], u, spare)
    dst = jnp.concatenate([n_pairs + TM + row, dst], axis=0)
    return tile_expert, n_used.reshape(1), src, dst


def _moe_kernel(te_ref, nu_ref, src_ref, src1_ref, src2_ref, dstp_ref, dstc_ref, x_hbm, g_ref,
                w1_ref, w3_ref, w2_ref, ys_hbm, xbuf, ybuf, xn_ref, gsem, ssem, zsem):
    del te_ref
    i = pl.program_id(0)
    last = pl.num_programs(0) - 1
    n_used = nu_ref[0]
    slot = i % 2
    other = 1 - slot
    gslot = i % GATHER_SLOTS
    gahead = (i + GATHER_SLOTS - 1) % GATHER_SLOTS
    n_pairs = ys_hbm.shape[0] - 2 * TM

    def gather_row(idx_ref, s, r):
        return pltpu.make_async_copy(x_hbm.at[idx_ref[0, 0, r]], xbuf.at[s, r], gsem.at[s])

    def scatter_row(idx_ref, s, r):
        return pltpu.make_async_copy(ybuf.at[s, r], ys_hbm.at[idx_ref[0, 0, r]], ssem.at[s])

    def rolled(row_copy):
        def body(r, carry):
            row_copy(r).start()
            return carry
        lax.fori_loop(0, TM, body, 0)

    def wait_gather(s):
        pltpu.make_async_copy(x_hbm.at[pl.ds(0, TM)], xbuf.at[s], gsem.at[s]).wait()

    def wait_scatter(s):
        pltpu.make_async_copy(ybuf.at[s], ys_hbm.at[pl.ds(0, TM)], ssem.at[s]).wait()

    @pl.when(i == 0)
    def _():
        rolled(lambda r: gather_row(src_ref, 0, r))
        rolled(lambda r: gather_row(src1_ref, 1, r))
        ybuf[1] = jnp.zeros((TM, *ROW_TILE), _F32)
        fill = pltpu.make_async_copy(ybuf.at[1], ys_hbm.at[pl.ds(n_pairs, TM)], zsem)
        fill.start()
        fill.wait()

    wait_gather(gslot)

    @pl.when(i < n_used)
    def _():
        xn_ref[...] = _rms(xbuf[gslot].reshape(TM, D_MODEL), g_ref[...]).astype(_BF16)

        n_pieces = _swiglu_pieces()

        def neighbour_rows(p):
            for r in range(p * TM // n_pieces, (p + 1) * TM // n_pieces):
                gather_row(src2_ref, gahead, r).start(priority=r % 2)
                scatter_row(dstp_ref, other, r).start(priority=r % 2)

        y = _swiglu(xn_ref, w1_ref.at[0, 0], w3_ref.at[0, 0], w2_ref.at[0, 0], side_work=neighbour_rows)

        @pl.when(i >= 1)
        def _():
            wait_scatter(slot)

        ybuf[slot] = jnp.concatenate(y, axis=1).reshape(TM, *ROW_TILE)

    @pl.when(i >= n_used)
    def _():
        @pl.when(i + GATHER_SLOTS - 1 <= last)
        def _():
            rolled(lambda r: gather_row(src2_ref, gahead, r))

        @pl.when(i == n_used)
        def _():
            rolled(lambda r: scatter_row(dstp_ref, other, r))

    @pl.when(i == last)
    def _():
        for back in range(GATHER_SLOTS - 1):
            @pl.when(last - back < n_used)
            def _():
                wait_gather((last - back + GATHER_SLOTS - 1) % GATHER_SLOTS)

        @pl.when(i < n_used)
        def _():
            rolled(lambda r: scatter_row(dstc_ref, slot, r))

        wait_scatter(0)
        wait_scatter(1)


def _moe(x, g, idx, w1, w3, w2, tile_expert, n_used, src, dst):
    n_tiles = src.shape[0]
    smem_rows = functools.partial(pl.BlockSpec, (1, 1, TM), memory_space=pltpu.SMEM)

    def expert_spec(shape):
        return pl.BlockSpec((1, 1, *shape), lambda i, te, *_: (idx, te[i], 0, 0),
                            pipeline_mode=pl.Buffered(1))

    grid_spec = pltpu.PrefetchScalarGridSpec(
        num_scalar_prefetch=2,
        grid=(n_tiles,),
        in_specs=[
            smem_rows(lambda i, *_: (i, 0, 0)),
            smem_rows(lambda i, *_: (jnp.minimum(i + 1, n_tiles - 1), 0, 0)),
            smem_rows(lambda i, *_: (jnp.minimum(i + GATHER_SLOTS - 1, n_tiles - 1), 0, 0)),
            smem_rows(lambda i, *_: (i, 0, 0)),
            smem_rows(lambda i, *_: (i + 1, 0, 0)),
            pl.BlockSpec(memory_space=pl.ANY),
            pl.BlockSpec((1, D_MODEL), lambda i, *_: (0, 0)),
            expert_spec((D_MODEL, D_FF)), expert_spec((D_MODEL, D_FF)), expert_spec((D_FF, D_MODEL)),
        ],
        out_specs=pl.BlockSpec(memory_space=pl.ANY),
        scratch_shapes=[pltpu.VMEM((GATHER_SLOTS, TM, *ROW_TILE), _F32),
                        pltpu.VMEM((2, TM, *ROW_TILE), _F32),
                        pltpu.VMEM((TM, D_MODEL), _BF16),
                        pltpu.SemaphoreType.DMA((GATHER_SLOTS,)),
                        pltpu.SemaphoreType.DMA((2,)),
                        pltpu.SemaphoreType.DMA(())],
    )
    return pl.pallas_call(
        _moe_kernel,
        grid_spec=grid_spec,
        out_shape=jax.ShapeDtypeStruct((TOP_K * x.shape[0] + 2 * TM, *ROW_TILE), _F32),
        compiler_params=pltpu.CompilerParams(
            dimension_semantics=("arbitrary",), vmem_limit_bytes=VMEM_LIMIT),
        name="moe",
    )(tile_expert, n_used, src, src, src, dst, dst, x, g, w1, w3, w2)


def _final_kernel(*refs):
    *in_refs, g_ref, o_ref = refs
    o_ref[...] = _rms(_residual_in(in_refs), g_ref[...])


def _final_norm(h, moe_out, g):
    n_rows = h.shape[0]
    arrays, descr = _layer_inputs(h, moe_out, n_rows)
    return pl.pallas_call(
        _final_kernel,
        grid=(n_rows // TM,),
        in_specs=_tile_specs(descr, n_rows, TM)[:len(descr)] + [_const_spec((1, D_MODEL))],
        out_specs=pl.BlockSpec((TM, D_MODEL), lambda i: (i, 0)),
        out_shape=jax.ShapeDtypeStruct((n_rows, D_MODEL), _F32),
        compiler_params=pltpu.CompilerParams(
            dimension_semantics=("arbitrary",), vmem_limit_bytes=VMEM_LIMIT),
        name="final_norm",
    )(*arrays, g)


def _row(v):
    return v.reshape(1, -1)


def kernel(x, norm_mix_g, norm_ffn_g, ev_w_in, ev_ws, ev_bs, ev_ln_a_g, ev_ln_a_b, ev_conv_w,
           ev_conv_b, ev_ln_b_g, ev_ln_b_b, ev_w_out, od_w_in, od_conv_w, od_w_out, ffn_w1,
           ffn_w3, ffn_w2, moe_router, moe_w1, moe_w3, moe_w2, final_g):
    bsz, seq, d = x.shape
    n_rows = bsz * seq
    depth = norm_mix_g.shape[0]
    assert d == D_MODEL and seq % TM == 0 and depth % 2 == 0 and TOP_K == 2

    h = x.reshape(n_rows, d)
    moe_out = None
    for layer in range(depth):
        i = layer // 2
        if layer % 2 == 0:
            p = dict(
                g=_row(norm_mix_g[layer]), w_in=ev_w_in, ws=ev_ws,
                bsb=jnp.broadcast_to(ev_bs[i][:, :, None], (A_HEADS, CHUNK, LANES)),
                ln_a_g=_row(ev_ln_a_g[i]), ln_a_b=_row(ev_ln_a_b[i]), conv_w=ev_conv_w[i],
                conv_b=_row(ev_conv_b[i]), ln_b_g=_row(ev_ln_b_g[i]), ln_b_b=_row(ev_ln_b_b[i]),
                w_out=ev_w_out)
            h = _mixer_ab(h, moe_out, n_rows, seq, i, p)
            h = _ffn(h, _row(norm_ffn_g[layer]), i, ffn_w1, ffn_w3, ffn_w2)
            moe_out = None
        else:
            router = jnp.pad(moe_router[i], ((0, 0), (0, LANES - N_EXPERTS)))
            r_hi = router.astype(_BF16)
            r_lo = (router - r_hi.astype(_F32)).astype(_BF16)
            p = dict(g=_row(norm_mix_g[layer]), w_in=od_w_in, conv_w=od_conv_w[i], w_out=od_w_out,
                     g_ffn=_row(norm_ffn_g[layer]), r_hi=r_hi, r_lo=r_lo)
            h, route = _mixer_c(h, seq, i, p)
            ys = _moe(h, _row(norm_ffn_g[layer]), i, moe_w1, moe_w3, moe_w2, *_route_plan(route, n_rows))
            moe_out = (ys, route)
    out = _final_norm(h, moe_out, _row(final_g))
    return out.reshape(bsz, seq, d)
```

```python
import functools

import jax
import jax.numpy as jnp
from jax import lax
from jax.experimental import pallas as pl
from jax.experimental.pallas import tpu as pltpu

D_MODEL = 1024
A_WIDTH = 512
A_HEADS = 4
CHUNK = 128
B_WIDTH = 512
CONV_B_WIDTH = 31
CONV_B_PAD = 15
CONV_C_WIDTH = 3
D_FF = 2816
N_EXPERTS = 8
TOP_K = 2
RMS_EPS = 1e-6
LN_EPS = 1e-5

LANES = 128
SUBLANES = 8
MXU_N = 256
D_BLOCKS = D_MODEL // LANES
assert D_BLOCKS == SUBLANES
TM = 512
HALO_B = 16
HALO_C = 8
FF_SPLITS = (0, 512, 1024, 1536, 2048, 2560, D_FF)
CONV_ROWS = 64
GATHER_SLOTS = 3
MIX_ROWS = 256
VMEM_LIMIT = 60 * 1024 * 1024
GATE_LANE = TOP_K

_F32 = jnp.float32
_BF16 = jnp.bfloat16


def _rms(x, g):
    y = x * lax.rsqrt(jnp.mean(x * x, axis=-1, keepdims=True) + RMS_EPS)
    return y * g


def _layernorm(x, g, b):
    mu = jnp.mean(x, axis=-1, keepdims=True)
    xc = x - mu
    y = xc * lax.rsqrt(jnp.mean(xc * xc, axis=-1, keepdims=True) + LN_EPS)
    return y * g + b


def _sigmoid(x):
    return 1.0 / (1.0 + jnp.exp(-x))


def _gelu_tanh(x):
    c = 0.7978845608028654
    return 0.5 * x * (1.0 + jnp.tanh(c * (x + 0.044715 * (x * x * x))))


def _dot(a, b):
    return jnp.dot(a, b, preferred_element_type=_F32)


def _residual_in(refs, rows=slice(None)):
    if len(refs) == 1:
        return refs[0][rows, :]
    x, ya, yb = (ref[rows].reshape(-1, D_MODEL) for ref in refs[:3])
    r = refs[3][rows, :]
    return x + r[:, GATE_LANE:GATE_LANE + 1] * ya + r[:, GATE_LANE + 1:GATE_LANE + 2] * yb


def _mixer_ab_kernel(n_in, tiles_per_seq, *refs):
    mains = refs[0:n_in]
    prevs = refs[n_in:2 * n_in]
    nexts = refs[2 * n_in:3 * n_in]
    (g_ref, w_in_f32, ws_f32, bsb_ref, lnag_ref, lnab_ref, cw_ref, cb_ref,
     lnbg_ref, lnbb_ref, w_out_f32, o_ref, glu_scr, shift_scr, mix_scr,
     w_in_ref, ws_ref, w_out_ref) = refs[3 * n_in:]

    i = pl.program_id(0)
    first = (i % tiles_per_seq) == 0
    last = (i % tiles_per_seq) == tiles_per_seq - 1

    @pl.when(i == 0)
    def _():
        w_in_ref[...] = w_in_f32[0].astype(_BF16)
        ws_ref[...] = ws_f32[0].astype(_BF16)
        w_out_ref[...] = w_out_f32[0].astype(_BF16)

    x = _residual_in(mains)
    g = g_ref[...]
    xn = _rms(x, g).astype(_BF16)
    z = _dot(xn, w_in_ref[...])

    xh = jnp.concatenate([_residual_in(prevs), _residual_in(nexts)], axis=0)
    xhn = _rms(xh, g).astype(_BF16)
    zh = _dot(xhn, w_in_ref[:, 2 * A_WIDTH:])
    glu_h = zh[:, :B_WIDTH] * _sigmoid(zh[:, B_WIDTH:])
    glu_scr[0:HALO_B, :] = jnp.where(first, 0.0, glu_h[:HALO_B])
    glu_scr[HALO_B + TM:, :] = jnp.where(last, 0.0, glu_h[HALO_B:])
    glu_scr[HALO_B:HALO_B + TM, :] = (
        z[:, 2 * A_WIDTH:2 * A_WIDTH + B_WIDTH] * _sigmoid(z[:, 2 * A_WIDTH + B_WIDTH:]))

    a_u = _gelu_tanh(z[:, :A_WIDTH])
    a_v = _layernorm(_gelu_tanh(z[:, A_WIDTH:2 * A_WIDTH]), lnag_ref[...], lnab_ref[...])
    a_vb = a_v.astype(_BF16)
    for c in range(TM // CHUNK):
        rows = slice(c * CHUNK, (c + 1) * CHUNK)
        for h in range(A_HEADS):
            cols = slice(h * LANES, (h + 1) * LANES)
            v = _dot(ws_ref[h], a_vb[rows, cols]) + bsb_ref[h]
            mix_scr[rows, cols] = (a_u[rows, cols] * v).astype(_BF16)

    n_shift_rows = TM + 2 * HALO_B - SUBLANES
    for s in range(1, SUBLANES):
        shift_scr[s - 1] = glu_scr[s:s + n_shift_rows, :]
    cb = cb_ref[...]
    for r in range(TM // CONV_ROWS):
        base = r * CONV_ROWS
        acc = jnp.zeros((CONV_ROWS, B_WIDTH), _F32)
        for k in range(CONV_B_WIDTH):
            off = base + HALO_B - CONV_B_PAD + k
            s = off % SUBLANES
            src = glu_scr if s == 0 else shift_scr.at[s - 1]
            acc = acc + src[off - s:off - s + CONV_ROWS, :] * cw_ref[k:k + 1, :]
        y = _layernorm(acc + cb, lnbg_ref[...], lnbb_ref[...])
        mix_scr[base:base + CONV_ROWS, A_WIDTH:] = (y * _sigmoid(y)).astype(_BF16)

    o_ref[...] = x + _dot(mix_scr[...], w_out_ref[...])


def _const_spec(shape):
    nd = len(shape)
    return pl.BlockSpec(shape, lambda *_: (0,) * nd, pipeline_mode=pl.Buffered(1))


def _layer_spec(shape, idx):
    nd = len(shape)
    return pl.BlockSpec((1, *shape), lambda *_: (idx,) + (0,) * nd, pipeline_mode=pl.Buffered(1))


ROW = (D_MODEL,)
ROW_TILE = (D_BLOCKS, LANES)


def _tile_specs(rows_and_offsets, n_rows, halo):
    nb = n_rows // halo
    per = TM // halo
    mains, prevs, nexts = [], [], []
    for row, off in rows_and_offsets:
        zeros = (0,) * len(row)
        mains.append(pl.BlockSpec((TM, *row), lambda i, o=off, z=zeros: (i + o // TM, *z)))
        prevs.append(pl.BlockSpec(
            (halo, *row), lambda i, o=off, z=zeros: (jnp.maximum(i * per - 1, 0) + o // halo, *z)))
        nexts.append(pl.BlockSpec(
            (halo, *row),
            lambda i, o=off, z=zeros: (jnp.minimum((i + 1) * per, nb - 1) + o // halo, *z)))
    return mains + prevs + nexts


def _layer_inputs(h, moe_out, n_rows):
    if moe_out is None:
        return [h], [(ROW, 0)]
    ys, route = moe_out
    return [h, ys, ys, route], [(ROW_TILE, 0), (ROW_TILE, 0), (ROW_TILE, n_rows), ((LANES,), 0)]


def _mixer_ab(h, moe_out, n_rows, seq, idx, p):
    arrays, descr = _layer_inputs(h, moe_out, n_rows)
    in_specs = _tile_specs(descr, n_rows, HALO_B) + [
        _const_spec((1, D_MODEL)),
        _layer_spec((D_MODEL, 2 * A_WIDTH + 2 * B_WIDTH), idx),
        _layer_spec((A_HEADS, CHUNK, CHUNK), idx),
        _const_spec((A_HEADS, CHUNK, LANES)),
        _const_spec((1, A_WIDTH)), _const_spec((1, A_WIDTH)),
        _const_spec((CONV_B_WIDTH, B_WIDTH)), _const_spec((1, B_WIDTH)),
        _const_spec((1, B_WIDTH)), _const_spec((1, B_WIDTH)),
        _layer_spec((A_WIDTH + B_WIDTH, D_MODEL), idx),
    ]
    return pl.pallas_call(
        functools.partial(_mixer_ab_kernel, len(arrays), seq // TM),
        grid=(n_rows // TM,),
        in_specs=in_specs,
        out_specs=pl.BlockSpec((TM, D_MODEL), lambda i: (i, 0)),
        out_shape=jax.ShapeDtypeStruct((n_rows, D_MODEL), _F32),
        scratch_shapes=[pltpu.VMEM((TM + 2 * HALO_B, B_WIDTH), _F32),
                        pltpu.VMEM((SUBLANES - 1, TM + 2 * HALO_B - SUBLANES, B_WIDTH), _F32),
                        pltpu.VMEM((TM, A_WIDTH + B_WIDTH), _BF16),
                        pltpu.VMEM((D_MODEL, 2 * A_WIDTH + 2 * B_WIDTH), _BF16),
                        pltpu.VMEM((A_HEADS, CHUNK, CHUNK), _BF16),
                        pltpu.VMEM((A_WIDTH + B_WIDTH, D_MODEL), _BF16)],
        compiler_params=pltpu.CompilerParams(
            dimension_semantics=("arbitrary",), vmem_limit_bytes=VMEM_LIMIT),
        name="mixer_ab",
    )(*(arrays * 3), p["g"], p["w_in"], p["ws"], p["bsb"], p["ln_a_g"], p["ln_a_b"],
      p["conv_w"], p["conv_b"], p["ln_b_g"], p["ln_b_b"], p["w_out"])


def _swiglu_pieces():
    n = 0
    for lo, hi in zip(FF_SPLITS[:-1], FF_SPLITS[1:]):
        n += 2 * ((hi - lo) // MXU_N) + D_MODEL // MXU_N
    return n


def _swiglu(xn_ref, w1_ref, w3_ref, w2_ref, side_work=None):
    piece = [0]

    def dot_cols(a, w_ref, rows, lo, hi):
        outs = []
        for c in range(lo, hi, MXU_N):
            if side_work is not None:
                side_work(piece[0])
            piece[0] += 1
            outs.append(_dot(a(), w_ref[rows, c:c + MXU_N].astype(_BF16)))
        return outs

    acc = None
    for lo, hi in zip(FF_SPLITS[:-1], FF_SPLITS[1:]):
        h1 = jnp.concatenate(dot_cols(lambda: xn_ref[...], w1_ref, slice(None), lo, hi), axis=1)
        h3 = jnp.concatenate(dot_cols(lambda: xn_ref[...], w3_ref, slice(None), lo, hi), axis=1)
        a = (h1 * _sigmoid(h1) * h3).astype(_BF16)
        d = dot_cols(lambda: a, w2_ref, slice(lo, hi), 0, D_MODEL)
        acc = d if acc is None else [p + q for p, q in zip(acc, d)]
    return acc


def _ffn_kernel(x_ref, g_ref, w1_ref, w3_ref, w2_ref, o_ref, xn_ref):
    x = x_ref[...]
    xn_ref[...] = _rms(x, g_ref[...]).astype(_BF16)
    y = _swiglu(xn_ref, w1_ref.at[0], w3_ref.at[0], w2_ref.at[0])
    o_ref[...] = x + jnp.concatenate(y, axis=1)


def _ffn(x, g, idx, w1, w3, w2):
    n_rows = x.shape[0]
    return pl.pallas_call(
        _ffn_kernel,
        grid=(n_rows // TM,),
        in_specs=[pl.BlockSpec((TM, D_MODEL), lambda i: (i, 0)),
                  _const_spec((1, D_MODEL)),
                  _layer_spec((D_MODEL, D_FF), idx), _layer_spec((D_MODEL, D_FF), idx),
                  _layer_spec((D_FF, D_MODEL), idx)],
        out_specs=pl.BlockSpec((TM, D_MODEL), lambda i: (i, 0)),
        out_shape=jax.ShapeDtypeStruct((n_rows, D_MODEL), _F32),
        scratch_shapes=[pltpu.VMEM((TM, D_MODEL), _BF16)],
        compiler_params=pltpu.CompilerParams(
            dimension_semantics=("arbitrary",), vmem_limit_bytes=VMEM_LIMIT),
        name="ffn",
    )(x, g, w1, w3, w2)


def _mixer_c_kernel(tiles_per_seq, x_ref, xp_ref, xq_ref, g_ref, w_in_f32, cw_ref, w_out_f32,
                    g2_ref, rcat_ref, o_ref, route_ref, u_scr, w_in_ref, w_out_ref):
    i = pl.program_id(0)
    first = (i % tiles_per_seq) == 0
    last = (i % tiles_per_seq) == tiles_per_seq - 1

    @pl.when(i == 0)
    def _():
        w_in_ref[...] = w_in_f32[0].astype(_BF16)
        w_out_ref[...] = w_out_f32[0].astype(_BF16)

    g = g_ref[...]
    xh = jnp.concatenate([xp_ref[...], xq_ref[...]], axis=0)
    xhn = _rms(xh, g).astype(_BF16)
    zh = _dot(xhn, w_in_ref[:, D_MODEL:])
    uh = zh[:, :D_MODEL] * zh[:, D_MODEL:]
    u_scr[0:HALO_C, :] = jnp.where(first, 0.0, uh[:HALO_C])
    u_scr[HALO_C + TM:, :] = jnp.where(last, 0.0, uh[HALO_C:])

    blocks = [slice(b * MIX_ROWS, (b + 1) * MIX_ROWS) for b in range(TM // MIX_ROWS)]
    gate_b = []
    for rows in blocks:
        xn = _rms(x_ref[rows, :], g).astype(_BF16)
        z = _dot(xn, w_in_ref[...])
        u_scr[HALO_C + rows.start:HALO_C + rows.stop, :] = z[:, D_MODEL:2 * D_MODEL] * z[:, 2 * D_MODEL:]
        gate_b.append(z[:, :D_MODEL])

    for rows, gb in zip(blocks, gate_b):
        lo_row = HALO_C + rows.start
        conv = (u_scr[lo_row - 1:lo_row - 1 + MIX_ROWS, :] * cw_ref[0:1, :]
                + u_scr[lo_row:lo_row + MIX_ROWS, :] * cw_ref[1:2, :]
                + u_scr[lo_row + 1:lo_row + 1 + MIX_ROWS, :] * cw_ref[2:3, :])
        y = (gb * conv).astype(_BF16)
        x1 = x_ref[rows, :] + _dot(y, w_out_ref[...])
        o_ref[rows] = x1.reshape(MIX_ROWS, *ROW_TILE)

        hn = _rms(x1, g2_ref[...])
        hi = hn.astype(_BF16)
        lo = (hn - hi.astype(_F32)).astype(_BF16)
        by_hi = _dot(hi, rcat_ref[...])
        logits = by_hi[:, :LANES] + (_dot(lo, rcat_ref[:, :LANES]) + by_hi[:, LANES:])

        lane = lax.broadcasted_iota(jnp.int32, logits.shape, 1).astype(_F32)
        neg = jnp.float32(-jnp.inf)
        lg = jnp.where(lane < N_EXPERTS, logits, neg)
        m1 = jnp.max(lg, axis=-1, keepdims=True)
        i1 = jnp.min(jnp.where(lg == m1, lane, float(LANES)), axis=-1, keepdims=True)
        lg2 = jnp.where(lane == i1, neg, lg)
        m2 = jnp.max(lg2, axis=-1, keepdims=True)
        i2 = jnp.min(jnp.where(lg2 == m2, lane, float(LANES)), axis=-1, keepdims=True)
        e2 = jnp.exp(m2 - m1)
        den = 1.0 + e2
        route_ref[rows, :] = jnp.where(lane == 0, i1, jnp.where(lane == 1, i2, jnp.where(
            lane == GATE_LANE, 1.0 / den, jnp.where(lane == GATE_LANE + 1, e2 / den, 0.0))))


def _mixer_c(x, seq, idx, p):
    n_rows = x.shape[0]
    in_specs = _tile_specs([(ROW, 0)], n_rows, HALO_C) + [
        _const_spec((1, D_MODEL)),
        _layer_spec((D_MODEL, 3 * D_MODEL), idx),
        _const_spec((CONV_C_WIDTH, D_MODEL)),
        _layer_spec((D_MODEL, D_MODEL), idx),
        _const_spec((1, D_MODEL)),
        _const_spec((D_MODEL, 2 * LANES)),
    ]
    return pl.pallas_call(
        functools.partial(_mixer_c_kernel, seq // TM),
        grid=(n_rows // TM,),
        in_specs=in_specs,
        out_specs=[pl.BlockSpec((TM, *ROW_TILE), lambda i: (i, 0, 0)),
                   pl.BlockSpec((TM, LANES), lambda i: (i, 0))],
        out_shape=[jax.ShapeDtypeStruct((n_rows, *ROW_TILE), _F32),
                   jax.ShapeDtypeStruct((n_rows, LANES), _F32)],
        scratch_shapes=[pltpu.VMEM((TM + 2 * HALO_C, D_MODEL), _F32),
                        pltpu.VMEM((D_MODEL, 3 * D_MODEL), _BF16),
                        pltpu.VMEM((D_MODEL, D_MODEL), _BF16)],
        compiler_params=pltpu.CompilerParams(
            dimension_semantics=("arbitrary",), vmem_limit_bytes=VMEM_LIMIT),
        name="mixer_c",
    )(x, x, x, p["g"], p["w_in"], p["conv_w"], p["w_out"], p["g_ffn"], p["r_cat"])


def _plan_kernel(te_ref, r0_ref, last_ref, cum_ref, u_ref):
    del te_ref
    i = pl.program_id(0)
    n_blocks = last_ref.shape[-1]
    rank = (r0_ref[i] + lax.broadcasted_iota(jnp.int32, (TM, 1), 0)).astype(_F32)
    blk = jnp.sum(jnp.where(last_ref[0] <= rank, 1.0, 0.0), axis=1, keepdims=True)
    block_ids = lax.broadcasted_iota(jnp.int32, (TM, n_blocks), 1).astype(_F32)
    onehot = jnp.where(block_ids == blk, 1.0, 0.0).astype(_BF16)
    digits = _dot(onehot, cum_ref[0])
    cum = digits[:, :LANES] * 256.0 + digits[:, LANES:]
    within = jnp.sum(jnp.where(cum <= rank, 1.0, 0.0), axis=1, keepdims=True)
    u = jnp.minimum(blk * LANES + within, n_blocks * LANES - 1.0)
    u_ref[0] = jnp.broadcast_to(u, (TM, LANES)).T[0:1, :].astype(jnp.int32)


def _route_plan(route, n_rows):
    n_pairs = TOP_K * n_rows
    n_blocks = n_pairs // LANES
    n_tiles = (n_pairs + N_EXPERTS * (TM - 1)) // TM
    e_cat = jnp.concatenate([route[:, k] for k in range(TOP_K)]).astype(jnp.int32)
    onehot = (e_cat[None, :] == jnp.arange(N_EXPERTS, dtype=jnp.int32)[:, None]).astype(jnp.int32)
    cum = jnp.cumsum(onehot, axis=1)
    counts = cum[:, -1]
    tiles_e = (counts + TM - 1) // TM
    tile_end = jnp.cumsum(tiles_e)
    n_used = tile_end[-1]
    tile_ids = jnp.arange(n_tiles, dtype=jnp.int32)
    tile_expert = jnp.sum(
        (tile_end[None, :] <= jnp.minimum(tile_ids, n_used - 1)[:, None]).astype(jnp.int32), axis=1)
    r0 = (tile_ids - (tile_end - tiles_e)[tile_expert]) * TM
    n_valid = jnp.where(tile_ids < n_used, jnp.clip(counts[tile_expert] - r0, 0, TM), 0)
    cum_blocks = cum.reshape(N_EXPERTS, n_blocks, LANES)
    block_last = cum_blocks[:, :, -1].astype(_F32).reshape(N_EXPERTS, 1, n_blocks)
    cum_digits = jnp.concatenate([cum_blocks // 256, cum_blocks % 256], axis=-1).astype(_BF16)

    u = pl.pallas_call(
        _plan_kernel,
        grid_spec=pltpu.PrefetchScalarGridSpec(
            num_scalar_prefetch=2,
            grid=(n_tiles,),
            in_specs=[pl.BlockSpec((1, 1, n_blocks), lambda i, te, *_: (te[i], 0, 0)),
                      pl.BlockSpec((1, n_blocks, 2 * LANES), lambda i, te, *_: (te[i], 0, 0))],
            out_specs=pl.BlockSpec((1, 1, TM), lambda i, *_: (i, 0, 0)),
        ),
        out_shape=jax.ShapeDtypeStruct((n_tiles, 1, TM), jnp.int32),
        compiler_params=pltpu.CompilerParams(
            dimension_semantics=("arbitrary",), vmem_limit_bytes=VMEM_LIMIT),
        name="plan",
    )(tile_expert, r0, block_last, cum_digits)
    src = jnp.where(u >= n_rows, u - n_rows, u)
    row = jnp.arange(TM, dtype=jnp.int32)[None, None, :]
    spare = n_pairs + (tile_ids % 2)[:, None, None] * TM + row
    dst = jnp.where(row < n_valid[:, None, None], u, spare)
    dst = jnp.concatenate([n_pairs + TM + row, dst], axis=0)
    return tile_expert, n_used.reshape(1), src, dst


def _moe_kernel(te_ref, nu_ref, src_ref, src1_ref, src2_ref, dstp_ref, dstc_ref, x_hbm, g_ref,
                w1_ref, w3_ref, w2_ref, ys_hbm, xbuf, ybuf, xn_ref, gsem, ssem, zsem):
    del te_ref
    i = pl.program_id(0)
    last = pl.num_programs(0) - 1
    n_used = nu_ref[0]
    slot = i % 2
    other = 1 - slot
    gslot = i % GATHER_SLOTS
    gahead = (i + GATHER_SLOTS - 1) % GATHER_SLOTS
    n_pairs = ys_hbm.shape[0] - 2 * TM

    def gather_row(idx_ref, s, r):
        return pltpu.make_async_copy(x_hbm.at[idx_ref[0, 0, r]], xbuf.at[s, r], gsem.at[s])

    def scatter_row(idx_ref, s, r):
        return pltpu.make_async_copy(ybuf.at[s, r], ys_hbm.at[idx_ref[0, 0, r]], ssem.at[s])

    def rolled(row_copy):
        def body(r, carry):
            row_copy(r).start()
            return carry
        lax.fori_loop(0, TM, body, 0)

    def wait_gather(s):
        pltpu.make_async_copy(x_hbm.at[pl.ds(0, TM)], xbuf.at[s], gsem.at[s]).wait()

    def wait_scatter(s):
        pltpu.make_async_copy(ybuf.at[s], ys_hbm.at[pl.ds(0, TM)], ssem.at[s]).wait()

    @pl.when(i == 0)
    def _():
        rolled(lambda r: gather_row(src_ref, 0, r))
        rolled(lambda r: gather_row(src1_ref, 1, r))
        ybuf[1] = jnp.zeros((TM, *ROW_TILE), _F32)
        fill = pltpu.make_async_copy(ybuf.at[1], ys_hbm.at[pl.ds(n_pairs, TM)], zsem)
        fill.start()
        fill.wait()

    wait_gather(gslot)

    @pl.when(i < n_used)
    def _():
        xn_ref[...] = _rms(xbuf[gslot].reshape(TM, D_MODEL), g_ref[...]).astype(_BF16)

        n_pieces = _swiglu_pieces()

        def neighbour_rows(p):
            for r in range(p * TM // n_pieces, (p + 1) * TM // n_pieces):
                gather_row(src2_ref, gahead, r).start(priority=r % 2)
                scatter_row(dstp_ref, other, r).start(priority=r % 2)

        y = _swiglu(xn_ref, w1_ref.at[0, 0], w3_ref.at[0, 0], w2_ref.at[0, 0], side_work=neighbour_rows)

        @pl.when(i >= 1)
        def _():
            wait_scatter(slot)

        ybuf[slot] = jnp.concatenate(y, axis=1).reshape(TM, *ROW_TILE)

    @pl.when(i >= n_used)
    def _():
        @pl.when(i + GATHER_SLOTS - 1 <= last)
        def _():
            rolled(lambda r: gather_row(src2_ref, gahead, r))

        @pl.when(i == n_used)
        def _():
            rolled(lambda r: scatter_row(dstp_ref, other, r))

    @pl.when(i == last)
    def _():
        for back in range(GATHER_SLOTS - 1):
            @pl.when(last - back < n_used)
            def _():
                wait_gather((last - back + GATHER_SLOTS - 1) % GATHER_SLOTS)

        @pl.when(i < n_used)
        def _():
            rolled(lambda r: scatter_row(dstc_ref, slot, r))

        wait_scatter(0)
        wait_scatter(1)


def _moe(x, g, idx, w1, w3, w2, tile_expert, n_used, src, dst):
    n_tiles = src.shape[0]
    smem_rows = functools.partial(pl.BlockSpec, (1, 1, TM), memory_space=pltpu.SMEM)

    def expert_spec(shape, buffers):
        return pl.BlockSpec((1, 1, *shape), lambda i, te, *_: (idx, te[i], 0, 0),
                            pipeline_mode=pl.Buffered(buffers))

    grid_spec = pltpu.PrefetchScalarGridSpec(
        num_scalar_prefetch=2,
        grid=(n_tiles,),
        in_specs=[
            smem_rows(lambda i, *_: (i, 0, 0)),
            smem_rows(lambda i, *_: (jnp.minimum(i + 1, n_tiles - 1), 0, 0)),
            smem_rows(lambda i, *_: (jnp.minimum(i + GATHER_SLOTS - 1, n_tiles - 1), 0, 0)),
            smem_rows(lambda i, *_: (i, 0, 0)),
            smem_rows(lambda i, *_: (i + 1, 0, 0)),
            pl.BlockSpec(memory_space=pl.ANY),
            pl.BlockSpec((1, D_MODEL), lambda i, *_: (0, 0)),
            expert_spec((D_MODEL, D_FF), 1), expert_spec((D_MODEL, D_FF), 1),
            expert_spec((D_FF, D_MODEL), 1),
        ],
        out_specs=pl.BlockSpec(memory_space=pl.ANY),
        scratch_shapes=[pltpu.VMEM((GATHER_SLOTS, TM, *ROW_TILE), _F32),
                        pltpu.VMEM((2, TM, *ROW_TILE), _F32),
                        pltpu.VMEM((TM, D_MODEL), _BF16),
                        pltpu.SemaphoreType.DMA((GATHER_SLOTS,)),
                        pltpu.SemaphoreType.DMA((2,)),
                        pltpu.SemaphoreType.DMA(())],
    )
    return pl.pallas_call(
        _moe_kernel,
        grid_spec=grid_spec,
        out_shape=jax.ShapeDtypeStruct((TOP_K * x.shape[0] + 2 * TM, *ROW_TILE), _F32),
        compiler_params=pltpu.CompilerParams(
            dimension_semantics=("arbitrary",), vmem_limit_bytes=VMEM_LIMIT),
        name="moe",
    )(tile_expert, n_used, src, src, src, dst, dst, x, g, w1, w3, w2)


def _final_kernel(*refs):
    *in_refs, g_ref, o_ref = refs
    o_ref[...] = _rms(_residual_in(in_refs), g_ref[...])


def _final_norm(h, moe_out, g):
    n_rows = h.shape[0]
    arrays, descr = _layer_inputs(h, moe_out, n_rows)
    return pl.pallas_call(
        _final_kernel,
        grid=(n_rows // TM,),
        in_specs=_tile_specs(descr, n_rows, TM)[:len(descr)] + [_const_spec((1, D_MODEL))],
        out_specs=pl.BlockSpec((TM, D_MODEL), lambda i: (i, 0)),
        out_shape=jax.ShapeDtypeStruct((n_rows, D_MODEL), _F32),
        compiler_params=pltpu.CompilerParams(
            dimension_semantics=("arbitrary",), vmem_limit_bytes=VMEM_LIMIT),
        name="final_norm",
    )(*arrays, g)


def _row(v):
    return v.reshape(1, -1)


def kernel(x, norm_mix_g, norm_ffn_g, ev_w_in, ev_ws, ev_bs, ev_ln_a_g, ev_ln_a_b, ev_conv_w,
           ev_conv_b, ev_ln_b_g, ev_ln_b_b, ev_w_out, od_w_in, od_conv_w, od_w_out, ffn_w1,
           ffn_w3, ffn_w2, moe_router, moe_w1, moe_w3, moe_w2, final_g):
    bsz, seq, d = x.shape
    n_rows = bsz * seq
    depth = norm_mix_g.shape[0]
    assert d == D_MODEL and seq % TM == 0 and depth % 2 == 0 and TOP_K == 2

    h = x.reshape(n_rows, d)
    moe_out = None
    for layer in range(depth):
        i = layer // 2
        if layer % 2 == 0:
            p = dict(
                g=_row(norm_mix_g[layer]), w_in=ev_w_in, ws=ev_ws,
                bsb=jnp.broadcast_to(ev_bs[i][:, :, None], (A_HEADS, CHUNK, LANES)),
                ln_a_g=_row(ev_ln_a_g[i]), ln_a_b=_row(ev_ln_a_b[i]), conv_w=ev_conv_w[i],
                conv_b=_row(ev_conv_b[i]), ln_b_g=_row(ev_ln_b_g[i]), ln_b_b=_row(ev_ln_b_b[i]),
                w_out=ev_w_out)
            h = _mixer_ab(h, moe_out, n_rows, seq, i, p)
            h = _ffn(h, _row(norm_ffn_g[layer]), i, ffn_w1, ffn_w3, ffn_w2)
            moe_out = None
        else:
            router = jnp.pad(moe_router[i], ((0, 0), (0, LANES - N_EXPERTS)))
            r_hi = router.astype(_BF16)
            r_lo = (router - r_hi.astype(_F32)).astype(_BF16)
            p = dict(g=_row(norm_mix_g[layer]), w_in=od_w_in, conv_w=od_conv_w[i], w_out=od_w_out,
                     g_ffn=_row(norm_ffn_g[layer]), r_cat=jnp.concatenate([r_hi, r_lo], axis=1))
            h, route = _mixer_c(h, seq, i, p)
            ys = _moe(h, _row(norm_ffn_g[layer]), i, moe_w1, moe_w3, moe_w2, *_route_plan(route, n_rows))
            moe_out = (ys, route)
    out = _final_norm(h, moe_out, _row(final_g))
    return out.reshape(bsz, seq, d)
```

```python
import functools

import jax
import jax.numpy as jnp
from jax import lax
from jax.experimental import pallas as pl
from jax.experimental.pallas import tpu as pltpu

D_MODEL = 1024
A_WIDTH = 512
A_HEADS = 4
CHUNK = 128
B_WIDTH = 512
CONV_B_WIDTH = 31
CONV_B_PAD = 15
CONV_C_WIDTH = 3
D_FF = 2816
N_EXPERTS = 8
TOP_K = 2
RMS_EPS = 1e-6
LN_EPS = 1e-5

LANES = 128
SUBLANES = 8
MXU_N = 256
D_BLOCKS = D_MODEL // LANES
assert D_BLOCKS == SUBLANES
TM = 512
TM_C = 1024
HALO_B = 16
HALO_C = 8
FF_SPLITS = (0, 512, 1024, 1536, 2048, 2560, D_FF)
CONV_ROWS = 64
GATHER_SLOTS = 3
MIX_ROWS = 256
VMEM_LIMIT = 60 * 1024 * 1024
GATE_LANE = TOP_K

_F32 = jnp.float32
_BF16 = jnp.bfloat16


def _rms(x, g):
    y = x * lax.rsqrt(jnp.mean(x * x, axis=-1, keepdims=True) + RMS_EPS)
    return y * g


def _layernorm(x, g, b):
    mu = jnp.mean(x, axis=-1, keepdims=True)
    xc = x - mu
    y = xc * lax.rsqrt(jnp.mean(xc * xc, axis=-1, keepdims=True) + LN_EPS)
    return y * g + b


def _sigmoid(x):
    return 1.0 / (1.0 + jnp.exp(-x))


def _gelu_tanh(x):
    c = 0.7978845608028654
    return 0.5 * x * (1.0 + jnp.tanh(c * (x + 0.044715 * (x * x * x))))


def _dot(a, b):
    return jnp.dot(a, b, preferred_element_type=_F32)


def _residual_in(refs, rows=slice(None)):
    if len(refs) == 1:
        return refs[0][rows, :]
    x, ya, yb = (ref[rows].reshape(-1, D_MODEL) for ref in refs[:3])
    r = refs[3][rows, :]
    return x + r[:, GATE_LANE:GATE_LANE + 1] * ya + r[:, GATE_LANE + 1:GATE_LANE + 2] * yb


def _mixer_ab_kernel(n_in, tiles_per_seq, *refs):
    mains = refs[0:n_in]
    prevs = refs[n_in:2 * n_in]
    nexts = refs[2 * n_in:3 * n_in]
    (g_ref, w_in_f32, ws_f32, bsb_ref, lnag_ref, lnab_ref, cw_ref, cb_ref,
     lnbg_ref, lnbb_ref, w_out_f32, o_ref, glu_scr, shift_scr, mix_scr,
     w_in_ref, ws_ref, w_out_ref) = refs[3 * n_in:]

    i = pl.program_id(0)
    first = (i % tiles_per_seq) == 0
    last = (i % tiles_per_seq) == tiles_per_seq - 1

    @pl.when(i == 0)
    def _():
        w_in_ref[...] = w_in_f32[0].astype(_BF16)
        ws_ref[...] = ws_f32[0].astype(_BF16)
        w_out_ref[...] = w_out_f32[0].astype(_BF16)

    x = _residual_in(mains)
    g = g_ref[...]
    xn = _rms(x, g).astype(_BF16)
    z = _dot(xn, w_in_ref[...])

    xh = jnp.concatenate([_residual_in(prevs), _residual_in(nexts)], axis=0)
    xhn = _rms(xh, g).astype(_BF16)
    zh = _dot(xhn, w_in_ref[:, 2 * A_WIDTH:])
    glu_h = zh[:, :B_WIDTH] * _sigmoid(zh[:, B_WIDTH:])
    glu_scr[0:HALO_B, :] = jnp.where(first, 0.0, glu_h[:HALO_B])
    glu_scr[HALO_B + TM:, :] = jnp.where(last, 0.0, glu_h[HALO_B:])
    glu_scr[HALO_B:HALO_B + TM, :] = (
        z[:, 2 * A_WIDTH:2 * A_WIDTH + B_WIDTH] * _sigmoid(z[:, 2 * A_WIDTH + B_WIDTH:]))

    a_u = _gelu_tanh(z[:, :A_WIDTH])
    a_v = _layernorm(_gelu_tanh(z[:, A_WIDTH:2 * A_WIDTH]), lnag_ref[...], lnab_ref[...])
    a_vb = a_v.astype(_BF16)
    for c in range(TM // CHUNK):
        rows = slice(c * CHUNK, (c + 1) * CHUNK)
        for h in range(A_HEADS):
            cols = slice(h * LANES, (h + 1) * LANES)
            v = _dot(ws_ref[h], a_vb[rows, cols]) + bsb_ref[h]
            mix_scr[rows, cols] = (a_u[rows, cols] * v).astype(_BF16)

    n_shift_rows = TM + 2 * HALO_B - SUBLANES
    for s in range(1, SUBLANES):
        shift_scr[s - 1] = glu_scr[s:s + n_shift_rows, :]
    cb = cb_ref[...]
    for r in range(TM // CONV_ROWS):
        base = r * CONV_ROWS
        acc = jnp.zeros((CONV_ROWS, B_WIDTH), _F32)
        for k in range(CONV_B_WIDTH):
            off = base + HALO_B - CONV_B_PAD + k
            s = off % SUBLANES
            src = glu_scr if s == 0 else shift_scr.at[s - 1]
            acc = acc + src[off - s:off - s + CONV_ROWS, :] * cw_ref[k:k + 1, :]
        y = _layernorm(acc + cb, lnbg_ref[...], lnbb_ref[...])
        mix_scr[base:base + CONV_ROWS, A_WIDTH:] = (y * _sigmoid(y)).astype(_BF16)

    o_ref[...] = x + _dot(mix_scr[...], w_out_ref[...])


def _const_spec(shape):
    nd = len(shape)
    return pl.BlockSpec(shape, lambda *_: (0,) * nd, pipeline_mode=pl.Buffered(1))


def _layer_spec(shape, idx):
    nd = len(shape)
    return pl.BlockSpec((1, *shape), lambda *_: (idx,) + (0,) * nd, pipeline_mode=pl.Buffered(1))


ROW = (D_MODEL,)
ROW_TILE = (D_BLOCKS, LANES)


def _tile_specs(rows_and_offsets, n_rows, halo, tm=TM):
    nb = n_rows // halo
    per = tm // halo
    mains, prevs, nexts = [], [], []
    for row, off in rows_and_offsets:
        zeros = (0,) * len(row)
        mains.append(pl.BlockSpec((tm, *row), lambda i, o=off, z=zeros: (i + o // tm, *z)))
        prevs.append(pl.BlockSpec(
            (halo, *row), lambda i, o=off, z=zeros: (jnp.maximum(i * per - 1, 0) + o // halo, *z)))
        nexts.append(pl.BlockSpec(
            (halo, *row),
            lambda i, o=off, z=zeros: (jnp.minimum((i + 1) * per, nb - 1) + o // halo, *z)))
    return mains + prevs + nexts


def _layer_inputs(h, moe_out, n_rows):
    if moe_out is None:
        return [h], [(ROW, 0)]
    ys, route = moe_out
    return [h, ys, ys, route], [(ROW_TILE, 0), (ROW_TILE, 0), (ROW_TILE, n_rows), ((LANES,), 0)]


def _mixer_ab(h, moe_out, n_rows, seq, idx, p):
    arrays, descr = _layer_inputs(h, moe_out, n_rows)
    in_specs = _tile_specs(descr, n_rows, HALO_B) + [
        _const_spec((1, D_MODEL)),
        _layer_spec((D_MODEL, 2 * A_WIDTH + 2 * B_WIDTH), idx),
        _layer_spec((A_HEADS, CHUNK, CHUNK), idx),
        _const_spec((A_HEADS, CHUNK, LANES)),
        _const_spec((1, A_WIDTH)), _const_spec((1, A_WIDTH)),
        _const_spec((CONV_B_WIDTH, B_WIDTH)), _const_spec((1, B_WIDTH)),
        _const_spec((1, B_WIDTH)), _const_spec((1, B_WIDTH)),
        _layer_spec((A_WIDTH + B_WIDTH, D_MODEL), idx),
    ]
    return pl.pallas_call(
        functools.partial(_mixer_ab_kernel, len(arrays), seq // TM),
        grid=(n_rows // TM,),
        in_specs=in_specs,
        out_specs=pl.BlockSpec((TM, D_MODEL), lambda i: (i, 0)),
        out_shape=jax.ShapeDtypeStruct((n_rows, D_MODEL), _F32),
        scratch_shapes=[pltpu.VMEM((TM + 2 * HALO_B, B_WIDTH), _F32),
                        pltpu.VMEM((SUBLANES - 1, TM + 2 * HALO_B - SUBLANES, B_WIDTH), _F32),
                        pltpu.VMEM((TM, A_WIDTH + B_WIDTH), _BF16),
                        pltpu.VMEM((D_MODEL, 2 * A_WIDTH + 2 * B_WIDTH), _BF16),
                        pltpu.VMEM((A_HEADS, CHUNK, CHUNK), _BF16),
                        pltpu.VMEM((A_WIDTH + B_WIDTH, D_MODEL), _BF16)],
        compiler_params=pltpu.CompilerParams(
            dimension_semantics=("arbitrary",), vmem_limit_bytes=VMEM_LIMIT),
        name="mixer_ab",
    )(*(arrays * 3), p["g"], p["w_in"], p["ws"], p["bsb"], p["ln_a_g"], p["ln_a_b"],
      p["conv_w"], p["conv_b"], p["ln_b_g"], p["ln_b_b"], p["w_out"])


def _swiglu_pieces():
    n = 0
    for lo, hi in zip(FF_SPLITS[:-1], FF_SPLITS[1:]):
        n += 2 * ((hi - lo) // MXU_N) + D_MODEL // MXU_N
    return n


def _swiglu(xn_ref, w1_ref, w3_ref, w2_ref, side_work=None):
    piece = [0]

    def dot_cols(a, w_ref, rows, lo, hi):
        outs = []
        for c in range(lo, hi, MXU_N):
            if side_work is not None:
                side_work(piece[0])
            piece[0] += 1
            outs.append(_dot(a(), w_ref[rows, c:c + MXU_N].astype(_BF16)))
        return outs

    acc = None
    for lo, hi in zip(FF_SPLITS[:-1], FF_SPLITS[1:]):
        h1 = jnp.concatenate(dot_cols(lambda: xn_ref[...], w1_ref, slice(None), lo, hi), axis=1)
        h3 = jnp.concatenate(dot_cols(lambda: xn_ref[...], w3_ref, slice(None), lo, hi), axis=1)
        a = (h1 * _sigmoid(h1) * h3).astype(_BF16)
        d = dot_cols(lambda: a, w2_ref, slice(lo, hi), 0, D_MODEL)
        acc = d if acc is None else [p + q for p, q in zip(acc, d)]
    return acc


def _ffn_kernel(x_ref, g_ref, w1_ref, w3_ref, w2_ref, o_ref, xn_ref):
    x = x_ref[...]
    xn_ref[...] = _rms(x, g_ref[...]).astype(_BF16)
    y = _swiglu(xn_ref, w1_ref.at[0], w3_ref.at[0], w2_ref.at[0])
    o_ref[...] = x + jnp.concatenate(y, axis=1)


def _ffn(x, g, idx, w1, w3, w2):
    n_rows = x.shape[0]
    return pl.pallas_call(
        _ffn_kernel,
        grid=(n_rows // TM,),
        in_specs=[pl.BlockSpec((TM, D_MODEL), lambda i: (i, 0)),
                  _const_spec((1, D_MODEL)),
                  _layer_spec((D_MODEL, D_FF), idx), _layer_spec((D_MODEL, D_FF), idx),
                  _layer_spec((D_FF, D_MODEL), idx)],
        out_specs=pl.BlockSpec((TM, D_MODEL), lambda i: (i, 0)),
        out_shape=jax.ShapeDtypeStruct((n_rows, D_MODEL), _F32),
        scratch_shapes=[pltpu.VMEM((TM, D_MODEL), _BF16)],
        compiler_params=pltpu.CompilerParams(
            dimension_semantics=("arbitrary",), vmem_limit_bytes=VMEM_LIMIT),
        name="ffn",
    )(x, g, w1, w3, w2)


def _mixer_c_kernel(tiles_per_seq, x_ref, xp_ref, xq_ref, g_ref, w_in_f32, cw_ref, w_out_f32,
                    g2_ref, rcat_ref, o_ref, route_ref, u_scr, w_in_ref, w_out_ref):
    i = pl.program_id(0)
    first = (i % tiles_per_seq) == 0
    last = (i % tiles_per_seq) == tiles_per_seq - 1

    @pl.when(i == 0)
    def _():
        w_in_ref[...] = w_in_f32[0].astype(_BF16)
        w_out_ref[...] = w_out_f32[0].astype(_BF16)

    g = g_ref[...]
    xh = jnp.concatenate([xp_ref[...], xq_ref[...]], axis=0)
    xhn = _rms(xh, g).astype(_BF16)
    zh = _dot(xhn, w_in_ref[:, D_MODEL:])
    uh = zh[:, :D_MODEL] * zh[:, D_MODEL:]
    u_scr[0:HALO_C, :] = jnp.where(first, 0.0, uh[:HALO_C])
    u_scr[HALO_C + TM_C:, :] = jnp.where(last, 0.0, uh[HALO_C:])

    blocks = [slice(b * MIX_ROWS, (b + 1) * MIX_ROWS) for b in range(TM_C // MIX_ROWS)]
    gate_b = []
    for rows in blocks:
        xn = _rms(x_ref[rows, :], g).astype(_BF16)
        z = _dot(xn, w_in_ref[...])
        u_scr[HALO_C + rows.start:HALO_C + rows.stop, :] = z[:, D_MODEL:2 * D_MODEL] * z[:, 2 * D_MODEL:]
        gate_b.append(z[:, :D_MODEL])

    for rows, gb in zip(blocks, gate_b):
        lo_row = HALO_C + rows.start
        conv = (u_scr[lo_row - 1:lo_row - 1 + MIX_ROWS, :] * cw_ref[0:1, :]
                + u_scr[lo_row:lo_row + MIX_ROWS, :] * cw_ref[1:2, :]
                + u_scr[lo_row + 1:lo_row + 1 + MIX_ROWS, :] * cw_ref[2:3, :])
        y = (gb * conv).astype(_BF16)
        x1 = x_ref[rows, :] + _dot(y, w_out_ref[...])
        o_ref[rows] = x1.reshape(MIX_ROWS, *ROW_TILE)

        hn = _rms(x1, g2_ref[...])
        hi = hn.astype(_BF16)
        lo = (hn - hi.astype(_F32)).astype(_BF16)
        by_hi = _dot(hi, rcat_ref[...])
        logits = by_hi[:, :LANES] + (_dot(lo, rcat_ref[:, :LANES]) + by_hi[:, LANES:])

        lane = lax.broadcasted_iota(jnp.int32, logits.shape, 1).astype(_F32)
        neg = jnp.float32(-jnp.inf)
        lg = jnp.where(lane < N_EXPERTS, logits, neg)
        m1 = jnp.max(lg, axis=-1, keepdims=True)
        i1 = jnp.min(jnp.where(lg == m1, lane, float(LANES)), axis=-1, keepdims=True)
        lg2 = jnp.where(lane == i1, neg, lg)
        m2 = jnp.max(lg2, axis=-1, keepdims=True)
        i2 = jnp.min(jnp.where(lg2 == m2, lane, float(LANES)), axis=-1, keepdims=True)
        e2 = jnp.exp(m2 - m1)
        den = 1.0 + e2
        route_ref[rows, :] = jnp.where(lane == 0, i1, jnp.where(lane == 1, i2, jnp.where(
            lane == GATE_LANE, 1.0 / den, jnp.where(lane == GATE_LANE + 1, e2 / den, 0.0))))


def _mixer_c(x, seq, idx, p):
    n_rows = x.shape[0]
    in_specs = _tile_specs([(ROW, 0)], n_rows, HALO_C, TM_C) + [
        _const_spec((1, D_MODEL)),
        _layer_spec((D_MODEL, 3 * D_MODEL), idx),
        _const_spec((CONV_C_WIDTH, D_MODEL)),
        _layer_spec((D_MODEL, D_MODEL), idx),
        _const_spec((1, D_MODEL)),
        _const_spec((D_MODEL, 2 * LANES)),
    ]
    return pl.pallas_call(
        functools.partial(_mixer_c_kernel, seq // TM_C),
        grid=(n_rows // TM_C,),
        in_specs=in_specs,
        out_specs=[pl.BlockSpec((TM_C, *ROW_TILE), lambda i: (i, 0, 0)),
                   pl.BlockSpec((TM_C, LANES), lambda i: (i, 0))],
        out_shape=[jax.ShapeDtypeStruct((n_rows, *ROW_TILE), _F32),
                   jax.ShapeDtypeStruct((n_rows, LANES), _F32)],
        scratch_shapes=[pltpu.VMEM((TM_C + 2 * HALO_C, D_MODEL), _F32),
                        pltpu.VMEM((D_MODEL, 3 * D_MODEL), _BF16),
                        pltpu.VMEM((D_MODEL, D_MODEL), _BF16)],
        compiler_params=pltpu.CompilerParams(
            dimension_semantics=("arbitrary",), vmem_limit_bytes=VMEM_LIMIT),
        name="mixer_c",
    )(x, x, x, p["g"], p["w_in"], p["conv_w"], p["w_out"], p["g_ffn"], p["r_cat"])


def _plan_kernel(te_ref, r0_ref, last_ref, cum_ref, u_ref):
    del te_ref
    i = pl.program_id(0)
    n_blocks = last_ref.shape[-1]
    rank = (r0_ref[i] + lax.broadcasted_iota(jnp.int32, (TM, 1), 0)).astype(_F32)
    blk = jnp.sum(jnp.where(last_ref[0] <= rank, 1.0, 0.0), axis=1, keepdims=True)
    block_ids = lax.broadcasted_iota(jnp.int32, (TM, n_blocks), 1).astype(_F32)
    onehot = jnp.where(block_ids == blk, 1.0, 0.0).astype(_BF16)
    digits = _dot(onehot, cum_ref[0])
    cum = digits[:, :LANES] * 256.0 + digits[:, LANES:]
    within = jnp.sum(jnp.where(cum <= rank, 1.0, 0.0), axis=1, keepdims=True)
    u = jnp.minimum(blk * LANES + within, n_blocks * LANES - 1.0)
    u_ref[0] = jnp.broadcast_to(u, (TM, LANES)).T[0:1, :].astype(jnp.int32)


def _route_plan(route, n_rows):
    n_pairs = TOP_K * n_rows
    n_blocks = n_pairs // LANES
    n_tiles = (n_pairs + N_EXPERTS * (TM - 1)) // TM
    e_cat = jnp.concatenate([route[:, k] for k in range(TOP_K)]).astype(jnp.int32)
    onehot = (e_cat[None, :] == jnp.arange(N_EXPERTS, dtype=jnp.int32)[:, None]).astype(jnp.int32)
    cum = jnp.cumsum(onehot, axis=1)
    counts = cum[:, -1]
    tiles_e = (counts + TM - 1) // TM
    tile_end = jnp.cumsum(tiles_e)
    n_used = tile_end[-1]
    tile_ids = jnp.arange(n_tiles, dtype=jnp.int32)
    tile_expert = jnp.sum(
        (tile_end[None, :] <= jnp.minimum(tile_ids, n_used - 1)[:, None]).astype(jnp.int32), axis=1)
    r0 = (tile_ids - (tile_end - tiles_e)[tile_expert]) * TM
    n_valid = jnp.where(tile_ids < n_used, jnp.clip(counts[tile_expert] - r0, 0, TM), 0)
    cum_blocks = cum.reshape(N_EXPERTS, n_blocks, LANES)
    block_last = cum_blocks[:, :, -1].astype(_F32).reshape(N_EXPERTS, 1, n_blocks)
    cum_digits = jnp.concatenate([cum_blocks // 256, cum_blocks % 256], axis=-1).astype(_BF16)

    u = pl.pallas_call(
        _plan_kernel,
        grid_spec=pltpu.PrefetchScalarGridSpec(
            num_scalar_prefetch=2,
            grid=(n_tiles,),
            in_specs=[pl.BlockSpec((1, 1, n_blocks), lambda i, te, *_: (te[i], 0, 0)),
                      pl.BlockSpec((1, n_blocks, 2 * LANES), lambda i, te, *_: (te[i], 0, 0))],
            out_specs=pl.BlockSpec((1, 1, TM), lambda i, *_: (i, 0, 0)),
        ),
        out_shape=jax.ShapeDtypeStruct((n_tiles, 1, TM), jnp.int32),
        compiler_params=pltpu.CompilerParams(
            dimension_semantics=("arbitrary",), vmem_limit_bytes=VMEM_LIMIT),
        name="plan",
    )(tile_expert, r0, block_last, cum_digits)
    src = jnp.where(u >= n_rows, u - n_rows, u)
    row = jnp.arange(TM, dtype=jnp.int32)[None, None, :]
    spare = n_pairs + (tile_ids % 2)[:, None, None] * TM + row
    dst = jnp.where(row < n_valid[:, None, None], u, spare)
    dst = jnp.concatenate([n_pairs + TM + row, dst], axis=0)
    return tile_expert, n_used.reshape(1), src, dst


def _moe_kernel(te_ref, nu_ref, src_ref, src1_ref, src2_ref, dstp_ref, dstc_ref, x_hbm, g_ref,
                w1_ref, w3_ref, w2_ref, ys_hbm, xbuf, ybuf, xn_ref, gsem, ssem, zsem):
    del te_ref
    i = pl.program_id(0)
    last = pl.num_programs(0) - 1
    n_used = nu_ref[0]
    slot = i % 2
    other = 1 - slot
    gslot = i % GATHER_SLOTS
    gahead = (i + GATHER_SLOTS - 1) % GATHER_SLOTS
    n_pairs = ys_hbm.shape[0] - 2 * TM

    def gather_row(idx_ref, s, r):
        return pltpu.make_async_copy(x_hbm.at[idx_ref[0, 0, r]], xbuf.at[s, r], gsem.at[s])

    def scatter_row(idx_ref, s, r):
        return pltpu.make_async_copy(ybuf.at[s, r], ys_hbm.at[idx_ref[0, 0, r]], ssem.at[s])

    def rolled(row_copy):
        def body(r, carry):
            row_copy(r).start()
            return carry
        lax.fori_loop(0, TM, body, 0)

    def wait_gather(s):
        pltpu.make_async_copy(x_hbm.at[pl.ds(0, TM)], xbuf.at[s], gsem.at[s]).wait()

    def wait_scatter(s):
        pltpu.make_async_copy(ybuf.at[s], ys_hbm.at[pl.ds(0, TM)], ssem.at[s]).wait()

    @pl.when(i == 0)
    def _():
        rolled(lambda r: gather_row(src_ref, 0, r))
        rolled(lambda r: gather_row(src1_ref, 1, r))
        ybuf[1] = jnp.zeros((TM, *ROW_TILE), _F32)
        fill = pltpu.make_async_copy(ybuf.at[1], ys_hbm.at[pl.ds(n_pairs, TM)], zsem)
        fill.start()
        fill.wait()

    wait_gather(gslot)

    @pl.when(i < n_used)
    def _():
        xn_ref[...] = _rms(xbuf[gslot].reshape(TM, D_MODEL), g_ref[...]).astype(_BF16)

        n_pieces = _swiglu_pieces()

        def neighbour_rows(p):
            for r in range(p * TM // n_pieces, (p + 1) * TM // n_pieces):
                gather_row(src2_ref, gahead, r).start(priority=r % 2)
                scatter_row(dstp_ref, other, r).start(priority=r % 2)

        y = _swiglu(xn_ref, w1_ref.at[0, 0], w3_ref.at[0, 0], w2_ref.at[0, 0], side_work=neighbour_rows)

        @pl.when(i >= 1)
        def _():
            wait_scatter(slot)

        ybuf[slot] = jnp.concatenate(y, axis=1).reshape(TM, *ROW_TILE)

    @pl.when(i >= n_used)
    def _():
        @pl.when(i + GATHER_SLOTS - 1 <= last)
        def _():
            rolled(lambda r: gather_row(src2_ref, gahead, r))

        @pl.when(i == n_used)
        def _():
            rolled(lambda r: scatter_row(dstp_ref, other, r))

    @pl.when(i == last)
    def _():
        for back in range(GATHER_SLOTS - 1):
            @pl.when(last - back < n_used)
            def _():
                wait_gather((last - back + GATHER_SLOTS - 1) % GATHER_SLOTS)

        @pl.when(i < n_used)
        def _():
            rolled(lambda r: scatter_row(dstc_ref, slot, r))

        wait_scatter(0)
        wait_scatter(1)


def _moe(x, g, idx, w1, w3, w2, tile_expert, n_used, src, dst):
    n_tiles = src.shape[0]
    smem_rows = functools.partial(pl.BlockSpec, (1, 1, TM), memory_space=pltpu.SMEM)

    def expert_spec(shape, buffers):
        return pl.BlockSpec((1, 1, *shape), lambda i, te, *_: (idx, te[i], 0, 0),
                            pipeline_mode=pl.Buffered(buffers))

    grid_spec = pltpu.PrefetchScalarGridSpec(
        num_scalar_prefetch=2,
        grid=(n_tiles,),
        in_specs=[
            smem_rows(lambda i, *_: (i, 0, 0)),
            smem_rows(lambda i, *_: (jnp.minimum(i + 1, n_tiles - 1), 0, 0)),
            smem_rows(lambda i, *_: (jnp.minimum(i + GATHER_SLOTS - 1, n_tiles - 1), 0, 0)),
            smem_rows(lambda i, *_: (i, 0, 0)),
            smem_rows(lambda i, *_: (i + 1, 0, 0)),
            pl.BlockSpec(memory_space=pl.ANY),
            pl.BlockSpec((1, D_MODEL), lambda i, *_: (0, 0)),
            expert_spec((D_MODEL, D_FF), 1), expert_spec((D_MODEL, D_FF), 1),
            expert_spec((D_FF, D_MODEL), 1),
        ],
        out_specs=pl.BlockSpec(memory_space=pl.ANY),
        scratch_shapes=[pltpu.VMEM((GATHER_SLOTS, TM, *ROW_TILE), _F32),
                        pltpu.VMEM((2, TM, *ROW_TILE), _F32),
                        pltpu.VMEM((TM, D_MODEL), _BF16),
                        pltpu.SemaphoreType.DMA((GATHER_SLOTS,)),
                        pltpu.SemaphoreType.DMA((2,)),
                        pltpu.SemaphoreType.DMA(())],
    )
    return pl.pallas_call(
        _moe_kernel,
        grid_spec=grid_spec,
        out_shape=jax.ShapeDtypeStruct((TOP_K * x.shape[0] + 2 * TM, *ROW_TILE), _F32),
        compiler_params=pltpu.CompilerParams(
            dimension_semantics=("arbitrary",), vmem_limit_bytes=VMEM_LIMIT),
        name="moe",
    )(tile_expert, n_used, src, src, src, dst, dst, x, g, w1, w3, w2)


def _final_kernel(*refs):
    *in_refs, g_ref, o_ref = refs
    o_ref[...] = _rms(_residual_in(in_refs), g_ref[...])


def _final_norm(h, moe_out, g):
    n_rows = h.shape[0]
    arrays, descr = _layer_inputs(h, moe_out, n_rows)
    return pl.pallas_call(
        _final_kernel,
        grid=(n_rows // TM,),
        in_specs=_tile_specs(descr, n_rows, TM)[:len(descr)] + [_const_spec((1, D_MODEL))],
        out_specs=pl.BlockSpec((TM, D_MODEL), lambda i: (i, 0)),
        out_shape=jax.ShapeDtypeStruct((n_rows, D_MODEL), _F32),
        compiler_params=pltpu.CompilerParams(
            dimension_semantics=("arbitrary",), vmem_limit_bytes=VMEM_LIMIT),
        name="final_norm",
    )(*arrays, g)


def _row(v):
    return v.reshape(1, -1)


def kernel(x, norm_mix_g, norm_ffn_g, ev_w_in, ev_ws, ev_bs, ev_ln_a_g, ev_ln_a_b, ev_conv_w,
           ev_conv_b, ev_ln_b_g, ev_ln_b_b, ev_w_out, od_w_in, od_conv_w, od_w_out, ffn_w1,
           ffn_w3, ffn_w2, moe_router, moe_w1, moe_w3, moe_w2, final_g):
    bsz, seq, d = x.shape
    n_rows = bsz * seq
    depth = norm_mix_g.shape[0]
    assert d == D_MODEL and seq % TM == 0 and seq % TM_C == 0 and depth % 2 == 0 and TOP_K == 2

    h = x.reshape(n_rows, d)
    moe_out = None
    for layer in range(depth):
        i = layer // 2
        if layer % 2 == 0:
            p = dict(
                g=_row(norm_mix_g[layer]), w_in=ev_w_in, ws=ev_ws,
                bsb=jnp.broadcast_to(ev_bs[i][:, :, None], (A_HEADS, CHUNK, LANES)),
                ln_a_g=_row(ev_ln_a_g[i]), ln_a_b=_row(ev_ln_a_b[i]), conv_w=ev_conv_w[i],
                conv_b=_row(ev_conv_b[i]), ln_b_g=_row(ev_ln_b_g[i]), ln_b_b=_row(ev_ln_b_b[i]),
                w_out=ev_w_out)
            h = _mixer_ab(h, moe_out, n_rows, seq, i, p)
            h = _ffn(h, _row(norm_ffn_g[layer]), i, ffn_w1, ffn_w3, ffn_w2)
            moe_out = None
        else:
            router = jnp.pad(moe_router[i], ((0, 0), (0, LANES - N_EXPERTS)))
            r_hi = router.astype(_BF16)
            r_lo = (router - r_hi.astype(_F32)).astype(_BF16)
            p = dict(g=_row(norm_mix_g[layer]), w_in=od_w_in, conv_w=od_conv_w[i], w_out=od_w_out,
                     g_ffn=_row(norm_ffn_g[layer]), r_cat=jnp.concatenate([r_hi, r_lo], axis=1))
            h, route = _mixer_c(h, seq, i, p)
            ys = _moe(h, _row(norm_ffn_g[layer]), i, moe_w1, moe_w3, moe_w2, *_route_plan(route, n_rows))
            moe_out = (ys, route)
    out = _final_norm(h, moe_out, _row(final_g))
    return out.reshape(bsz, seq, d)
```

```python
import functools

import jax
import jax.numpy as jnp
from jax import lax
from jax.experimental import pallas as pl
from jax.experimental.pallas import tpu as pltpu

D_MODEL = 1024
A_WIDTH = 512
A_HEADS = 4
CHUNK = 128
B_WIDTH = 512
CONV_B_WIDTH = 31
CONV_B_PAD = 15
CONV_C_WIDTH = 3
D_FF = 2816
N_EXPERTS = 8
TOP_K = 2
RMS_EPS = 1e-6
LN_EPS = 1e-5

LANES = 128
SUBLANES = 8
MXU_N = 256
D_BLOCKS = D_MODEL // LANES
assert D_BLOCKS == SUBLANES
TM = 512
TM_C = 1024
HALO_B = 16
HALO_C = 8
FF_SPLITS = (0, 512, 1024, 1536, 2048, 2560, D_FF)
CONV_ROWS = 128
GATHER_SLOTS = 3
MIX_ROWS = 512
VMEM_LIMIT = 60 * 1024 * 1024
GATE_LANE = TOP_K

_F32 = jnp.float32
_BF16 = jnp.bfloat16


def _rms(x, g):
    y = x * lax.rsqrt(jnp.mean(x * x, axis=-1, keepdims=True) + RMS_EPS)
    return y * g


def _layernorm(x, g, b):
    mu = jnp.mean(x, axis=-1, keepdims=True)
    xc = x - mu
    y = xc * lax.rsqrt(jnp.mean(xc * xc, axis=-1, keepdims=True) + LN_EPS)
    return y * g + b


def _sigmoid(x):
    return 1.0 / (1.0 + jnp.exp(-x))


def _gelu_tanh(x):
    c = 0.7978845608028654
    return 0.5 * x * (1.0 + jnp.tanh(c * (x + 0.044715 * (x * x * x))))


def _dot(a, b):
    return jnp.dot(a, b, preferred_element_type=_F32)


def _residual_in(refs, rows=slice(None)):
    if len(refs) == 1:
        return refs[0][rows, :]
    x, ya, yb = (ref[rows].reshape(-1, D_MODEL) for ref in refs[:3])
    r = refs[3][rows, :]
    return x + r[:, GATE_LANE:GATE_LANE + 1] * ya + r[:, GATE_LANE + 1:GATE_LANE + 2] * yb


def _mixer_ab_kernel(n_in, tiles_per_seq, *refs):
    mains = refs[0:n_in]
    prevs = refs[n_in:2 * n_in]
    nexts = refs[2 * n_in:3 * n_in]
    (g_ref, w_in_f32, ws_f32, bsb_ref, lnag_ref, lnab_ref, cw_ref, cb_ref,
     lnbg_ref, lnbb_ref, w_out_f32, o_ref, glu_scr, shift_scr, mix_scr,
     w_in_ref, ws_ref, w_out_ref) = refs[3 * n_in:]

    i = pl.program_id(0)
    first = (i % tiles_per_seq) == 0
    last = (i % tiles_per_seq) == tiles_per_seq - 1

    @pl.when(i == 0)
    def _():
        w_in_ref[...] = w_in_f32[0].astype(_BF16)
        ws_ref[...] = ws_f32[0].astype(_BF16)
        w_out_ref[...] = w_out_f32[0].astype(_BF16)

    x = _residual_in(mains)
    g = g_ref[...]
    xn = _rms(x, g).astype(_BF16)
    z = _dot(xn, w_in_ref[...])

    xh = jnp.concatenate([_residual_in(prevs), _residual_in(nexts)], axis=0)
    xhn = _rms(xh, g).astype(_BF16)
    zh = _dot(xhn, w_in_ref[:, 2 * A_WIDTH:])
    glu_h = zh[:, :B_WIDTH] * _sigmoid(zh[:, B_WIDTH:])
    glu_scr[0:HALO_B, :] = jnp.where(first, 0.0, glu_h[:HALO_B])
    glu_scr[HALO_B + TM:, :] = jnp.where(last, 0.0, glu_h[HALO_B:])
    glu_scr[HALO_B:HALO_B + TM, :] = (
        z[:, 2 * A_WIDTH:2 * A_WIDTH + B_WIDTH] * _sigmoid(z[:, 2 * A_WIDTH + B_WIDTH:]))

    a_u = _gelu_tanh(z[:, :A_WIDTH])
    a_v = _layernorm(_gelu_tanh(z[:, A_WIDTH:2 * A_WIDTH]), lnag_ref[...], lnab_ref[...])
    a_vb = a_v.astype(_BF16)
    for c in range(TM // CHUNK):
        rows = slice(c * CHUNK, (c + 1) * CHUNK)
        for h in range(A_HEADS):
            cols = slice(h * LANES, (h + 1) * LANES)
            v = _dot(ws_ref[h], a_vb[rows, cols]) + bsb_ref[h]
            mix_scr[rows, cols] = (a_u[rows, cols] * v).astype(_BF16)

    n_shift_rows = TM + 2 * HALO_B - SUBLANES
    for s in range(1, SUBLANES):
        shift_scr[s - 1] = glu_scr[s:s + n_shift_rows, :]
    cb = cb_ref[...]
    for r in range(TM // CONV_ROWS):
        base = r * CONV_ROWS
        acc = jnp.zeros((CONV_ROWS, B_WIDTH), _F32)
        for k in range(CONV_B_WIDTH):
            off = base + HALO_B - CONV_B_PAD + k
            s = off % SUBLANES
            src = glu_scr if s == 0 else shift_scr.at[s - 1]
            acc = acc + src[off - s:off - s + CONV_ROWS, :] * cw_ref[k:k + 1, :]
        y = _layernorm(acc + cb, lnbg_ref[...], lnbb_ref[...])
        mix_scr[base:base + CONV_ROWS, A_WIDTH:] = (y * _sigmoid(y)).astype(_BF16)

    o_ref[...] = x + _dot(mix_scr[...], w_out_ref[...])


def _const_spec(shape):
    nd = len(shape)
    return pl.BlockSpec(shape, lambda *_: (0,) * nd, pipeline_mode=pl.Buffered(1))


def _layer_spec(shape, idx):
    nd = len(shape)
    return pl.BlockSpec((1, *shape), lambda *_: (idx,) + (0,) * nd, pipeline_mode=pl.Buffered(1))


ROW = (D_MODEL,)
ROW_TILE = (D_BLOCKS, LANES)


def _tile_specs(rows_and_offsets, n_rows, halo, tm=TM):
    nb = n_rows // halo
    per = tm // halo
    mains, prevs, nexts = [], [], []
    for row, off in rows_and_offsets:
        zeros = (0,) * len(row)
        mains.append(pl.BlockSpec((tm, *row), lambda i, o=off, z=zeros: (i + o // tm, *z)))
        prevs.append(pl.BlockSpec(
            (halo, *row), lambda i, o=off, z=zeros: (jnp.maximum(i * per - 1, 0) + o // halo, *z)))
        nexts.append(pl.BlockSpec(
            (halo, *row),
            lambda i, o=off, z=zeros: (jnp.minimum((i + 1) * per, nb - 1) + o // halo, *z)))
    return mains + prevs + nexts


def _layer_inputs(h, moe_out, n_rows):
    if moe_out is None:
        return [h], [(ROW, 0)]
    ys, route = moe_out
    return [h, ys, ys, route], [(ROW_TILE, 0), (ROW_TILE, 0), (ROW_TILE, n_rows), ((LANES,), 0)]


def _mixer_ab(h, moe_out, n_rows, seq, idx, p):
    arrays, descr = _layer_inputs(h, moe_out, n_rows)
    in_specs = _tile_specs(descr, n_rows, HALO_B) + [
        _const_spec((1, D_MODEL)),
        _layer_spec((D_MODEL, 2 * A_WIDTH + 2 * B_WIDTH), idx),
        _layer_spec((A_HEADS, CHUNK, CHUNK), idx),
        _const_spec((A_HEADS, CHUNK, LANES)),
        _const_spec((1, A_WIDTH)), _const_spec((1, A_WIDTH)),
        _const_spec((CONV_B_WIDTH, B_WIDTH)), _const_spec((1, B_WIDTH)),
        _const_spec((1, B_WIDTH)), _const_spec((1, B_WIDTH)),
        _layer_spec((A_WIDTH + B_WIDTH, D_MODEL), idx),
    ]
    return pl.pallas_call(
        functools.partial(_mixer_ab_kernel, len(arrays), seq // TM),
        grid=(n_rows // TM,),
        in_specs=in_specs,
        out_specs=pl.BlockSpec((TM, D_MODEL), lambda i: (i, 0)),
        out_shape=jax.ShapeDtypeStruct((n_rows, D_MODEL), _F32),
        scratch_shapes=[pltpu.VMEM((TM + 2 * HALO_B, B_WIDTH), _F32),
                        pltpu.VMEM((SUBLANES - 1, TM + 2 * HALO_B - SUBLANES, B_WIDTH), _F32),
                        pltpu.VMEM((TM, A_WIDTH + B_WIDTH), _BF16),
                        pltpu.VMEM((D_MODEL, 2 * A_WIDTH + 2 * B_WIDTH), _BF16),
                        pltpu.VMEM((A_HEADS, CHUNK, CHUNK), _BF16),
                        pltpu.VMEM((A_WIDTH + B_WIDTH, D_MODEL), _BF16)],
        compiler_params=pltpu.CompilerParams(
            dimension_semantics=("arbitrary",), vmem_limit_bytes=VMEM_LIMIT),
        name="mixer_ab",
    )(*(arrays * 3), p["g"], p["w_in"], p["ws"], p["bsb"], p["ln_a_g"], p["ln_a_b"],
      p["conv_w"], p["conv_b"], p["ln_b_g"], p["ln_b_b"], p["w_out"])


def _swiglu_pieces():
    n = 0
    for lo, hi in zip(FF_SPLITS[:-1], FF_SPLITS[1:]):
        n += 2 * ((hi - lo) // MXU_N) + D_MODEL // MXU_N
    return n


def _swiglu(xn_ref, w1_ref, w3_ref, w2_ref, side_work=None):
    piece = [0]

    def dot_cols(a, w_ref, rows, lo, hi):
        outs = []
        for c in range(lo, hi, MXU_N):
            if side_work is not None:
                side_work(piece[0])
            piece[0] += 1
            outs.append(_dot(a(), w_ref[rows, c:c + MXU_N].astype(_BF16)))
        return outs

    acc = None
    for lo, hi in zip(FF_SPLITS[:-1], FF_SPLITS[1:]):
        h1 = jnp.concatenate(dot_cols(lambda: xn_ref[...], w1_ref, slice(None), lo, hi), axis=1)
        h3 = jnp.concatenate(dot_cols(lambda: xn_ref[...], w3_ref, slice(None), lo, hi), axis=1)
        a = (h1 * _sigmoid(h1) * h3).astype(_BF16)
        d = dot_cols(lambda: a, w2_ref, slice(lo, hi), 0, D_MODEL)
        acc = d if acc is None else [p + q for p, q in zip(acc, d)]
    return acc


def _ffn_kernel(x_ref, g_ref, w1_ref, w3_ref, w2_ref, o_ref, xn_ref):
    x = x_ref[...]
    xn_ref[...] = _rms(x, g_ref[...]).astype(_BF16)
    y = _swiglu(xn_ref, w1_ref.at[0], w3_ref.at[0], w2_ref.at[0])
    o_ref[...] = x + jnp.concatenate(y, axis=1)


def _ffn(x, g, idx, w1, w3, w2):
    n_rows = x.shape[0]
    return pl.pallas_call(
        _ffn_kernel,
        grid=(n_rows // TM,),
        in_specs=[pl.BlockSpec((TM, D_MODEL), lambda i: (i, 0)),
                  _const_spec((1, D_MODEL)),
                  _layer_spec((D_MODEL, D_FF), idx), _layer_spec((D_MODEL, D_FF), idx),
                  _layer_spec((D_FF, D_MODEL), idx)],
        out_specs=pl.BlockSpec((TM, D_MODEL), lambda i: (i, 0)),
        out_shape=jax.ShapeDtypeStruct((n_rows, D_MODEL), _F32),
        scratch_shapes=[pltpu.VMEM((TM, D_MODEL), _BF16)],
        compiler_params=pltpu.CompilerParams(
            dimension_semantics=("arbitrary",), vmem_limit_bytes=VMEM_LIMIT),
        name="ffn",
    )(x, g, w1, w3, w2)


def _mixer_c_kernel(tiles_per_seq, x_ref, xp_ref, xq_ref, g_ref, w_in_f32, cw_ref, w_out_f32,
                    g2_ref, rcat_ref, o_ref, route_ref, u_scr, w_in_ref, w_out_ref):
    i = pl.program_id(0)
    first = (i % tiles_per_seq) == 0
    last = (i % tiles_per_seq) == tiles_per_seq - 1

    @pl.when(i == 0)
    def _():
        w_in_ref[...] = w_in_f32[0].astype(_BF16)
        w_out_ref[...] = w_out_f32[0].astype(_BF16)

    g = g_ref[...]
    xh = jnp.concatenate([xp_ref[...], xq_ref[...]], axis=0)
    xhn = _rms(xh, g).astype(_BF16)
    zh = _dot(xhn, w_in_ref[:, D_MODEL:])
    uh = zh[:, :D_MODEL] * zh[:, D_MODEL:]
    u_scr[0:HALO_C, :] = jnp.where(first, 0.0, uh[:HALO_C])
    u_scr[HALO_C + TM_C:, :] = jnp.where(last, 0.0, uh[HALO_C:])

    blocks = [slice(b * MIX_ROWS, (b + 1) * MIX_ROWS) for b in range(TM_C // MIX_ROWS)]
    gate_b = []
    for rows in blocks:
        xn = _rms(x_ref[rows, :], g).astype(_BF16)
        z = _dot(xn, w_in_ref[...])
        u_scr[HALO_C + rows.start:HALO_C + rows.stop, :] = z[:, D_MODEL:2 * D_MODEL] * z[:, 2 * D_MODEL:]
        gate_b.append(z[:, :D_MODEL])

    for rows, gb in zip(blocks, gate_b):
        lo_row = HALO_C + rows.start
        conv = (u_scr[lo_row - 1:lo_row - 1 + MIX_ROWS, :] * cw_ref[0:1, :]
                + u_scr[lo_row:lo_row + MIX_ROWS, :] * cw_ref[1:2, :]
                + u_scr[lo_row + 1:lo_row + 1 + MIX_ROWS, :] * cw_ref[2:3, :])
        y = (gb * conv).astype(_BF16)
        x1 = x_ref[rows, :] + _dot(y, w_out_ref[...])
        o_ref[rows] = x1.reshape(MIX_ROWS, *ROW_TILE)

        hn = _rms(x1, g2_ref[...])
        hi = hn.astype(_BF16)
        lo = (hn - hi.astype(_F32)).astype(_BF16)
        by_hi = _dot(hi, rcat_ref[...])
        logits = by_hi[:, :LANES] + (_dot(lo, rcat_ref[:, :LANES]) + by_hi[:, LANES:])

        lane = lax.broadcasted_iota(jnp.int32, logits.shape, 1).astype(_F32)
        neg = jnp.float32(-jnp.inf)
        lg = jnp.where(lane < N_EXPERTS, logits, neg)
        m1 = jnp.max(lg, axis=-1, keepdims=True)
        i1 = jnp.min(jnp.where(lg == m1, lane, float(LANES)), axis=-1, keepdims=True)
        lg2 = jnp.where(lane == i1, neg, lg)
        m2 = jnp.max(lg2, axis=-1, keepdims=True)
        i2 = jnp.min(jnp.where(lg2 == m2, lane, float(LANES)), axis=-1, keepdims=True)
        e2 = jnp.exp(m2 - m1)
        den = 1.0 + e2
        route_ref[rows, :] = jnp.where(lane == 0, i1, jnp.where(lane == 1, i2, jnp.where(
            lane == GATE_LANE, 1.0 / den, jnp.where(lane == GATE_LANE + 1, e2 / den, 0.0))))


def _mixer_c(x, seq, idx, p):
    n_rows = x.shape[0]
    in_specs = _tile_specs([(ROW, 0)], n_rows, HALO_C, TM_C) + [
        _const_spec((1, D_MODEL)),
        _layer_spec((D_MODEL, 3 * D_MODEL), idx),
        _const_spec((CONV_C_WIDTH, D_MODEL)),
        _layer_spec((D_MODEL, D_MODEL), idx),
        _const_spec((1, D_MODEL)),
        _const_spec((D_MODEL, 2 * LANES)),
    ]
    return pl.pallas_call(
        functools.partial(_mixer_c_kernel, seq // TM_C),
        grid=(n_rows // TM_C,),
        in_specs=in_specs,
        out_specs=[pl.BlockSpec((TM_C, *ROW_TILE), lambda i: (i, 0, 0)),
                   pl.BlockSpec((TM_C, LANES), lambda i: (i, 0))],
        out_shape=[jax.ShapeDtypeStruct((n_rows, *ROW_TILE), _F32),
                   jax.ShapeDtypeStruct((n_rows, LANES), _F32)],
        scratch_shapes=[pltpu.VMEM((TM_C + 2 * HALO_C, D_MODEL), _F32),
                        pltpu.VMEM((D_MODEL, 3 * D_MODEL), _BF16),
                        pltpu.VMEM((D_MODEL, D_MODEL), _BF16)],
        compiler_params=pltpu.CompilerParams(
            dimension_semantics=("arbitrary",), vmem_limit_bytes=VMEM_LIMIT),
        name="mixer_c",
    )(x, x, x, p["g"], p["w_in"], p["conv_w"], p["w_out"], p["g_ffn"], p["r_cat"])


def _plan_kernel(te_ref, r0_ref, last_ref, cum_ref, u_ref):
    del te_ref
    i = pl.program_id(0)
    n_blocks = last_ref.shape[-1]
    rank = (r0_ref[i] + lax.broadcasted_iota(jnp.int32, (TM, 1), 0)).astype(_F32)
    blk = jnp.sum(jnp.where(last_ref[0] <= rank, 1.0, 0.0), axis=1, keepdims=True)
    block_ids = lax.broadcasted_iota(jnp.int32, (TM, n_blocks), 1).astype(_F32)
    onehot = jnp.where(block_ids == blk, 1.0, 0.0).astype(_BF16)
    digits = _dot(onehot, cum_ref[0])
    cum = digits[:, :LANES] * 256.0 + digits[:, LANES:]
    within = jnp.sum(jnp.where(cum <= rank, 1.0, 0.0), axis=1, keepdims=True)
    u = jnp.minimum(blk * LANES + within, n_blocks * LANES - 1.0)
    u_ref[0] = jnp.broadcast_to(u, (TM, LANES)).T[0:1, :].astype(jnp.int32)


def _route_plan(route, n_rows):
    n_pairs = TOP_K * n_rows
    n_blocks = n_pairs // LANES
    n_tiles = (n_pairs + N_EXPERTS * (TM - 1)) // TM
    e_cat = jnp.concatenate([route[:, k] for k in range(TOP_K)]).astype(jnp.int32)
    onehot = (e_cat[None, :] == jnp.arange(N_EXPERTS, dtype=jnp.int32)[:, None]).astype(jnp.int32)
    cum = jnp.cumsum(onehot, axis=1)
    counts = cum[:, -1]
    tiles_e = (counts + TM - 1) // TM
    tile_end = jnp.cumsum(tiles_e)
    n_used = tile_end[-1]
    tile_ids = jnp.arange(n_tiles, dtype=jnp.int32)
    tile_expert = jnp.sum(
        (tile_end[None, :] <= jnp.minimum(tile_ids, n_used - 1)[:, None]).astype(jnp.int32), axis=1)
    r0 = (tile_ids - (tile_end - tiles_e)[tile_expert]) * TM
    n_valid = jnp.where(tile_ids < n_used, jnp.clip(counts[tile_expert] - r0, 0, TM), 0)
    cum_blocks = cum.reshape(N_EXPERTS, n_blocks, LANES)
    block_last = cum_blocks[:, :, -1].astype(_F32).reshape(N_EXPERTS, 1, n_blocks)
    cum_digits = jnp.concatenate([cum_blocks // 256, cum_blocks % 256], axis=-1).astype(_BF16)

    u = pl.pallas_call(
        _plan_kernel,
        grid_spec=pltpu.PrefetchScalarGridSpec(
            num_scalar_prefetch=2,
            grid=(n_tiles,),
            in_specs=[pl.BlockSpec((1, 1, n_blocks), lambda i, te, *_: (te[i], 0, 0)),
                      pl.BlockSpec((1, n_blocks, 2 * LANES), lambda i, te, *_: (te[i], 0, 0))],
            out_specs=pl.BlockSpec((1, 1, TM), lambda i, *_: (i, 0, 0)),
        ),
        out_shape=jax.ShapeDtypeStruct((n_tiles, 1, TM), jnp.int32),
        compiler_params=pltpu.CompilerParams(
            dimension_semantics=("arbitrary",), vmem_limit_bytes=VMEM_LIMIT),
        name="plan",
    )(tile_expert, r0, block_last, cum_digits)
    src = jnp.where(u >= n_rows, u - n_rows, u)
    row = jnp.arange(TM, dtype=jnp.int32)[None, None, :]
    spare = n_pairs + (tile_ids % 2)[:, None, None] * TM + row
    dst = jnp.where(row < n_valid[:, None, None], u, spare)
    dst = jnp.concatenate([n_pairs + TM + row, dst], axis=0)
    return tile_expert, n_used.reshape(1), src, dst


def _moe_kernel(te_ref, nu_ref, src_ref, src1_ref, src2_ref, dstp_ref, dstc_ref, x_hbm, g_ref,
                w1_ref, w3_ref, w2_ref, ys_hbm, xbuf, ybuf, xn_ref, gsem, ssem, zsem):
    del te_ref
    i = pl.program_id(0)
    last = pl.num_programs(0) - 1
    n_used = nu_ref[0]
    slot = i % 2
    other = 1 - slot
    gslot = i % GATHER_SLOTS
    gahead = (i + GATHER_SLOTS - 1) % GATHER_SLOTS
    n_pairs = ys_hbm.shape[0] - 2 * TM

    def gather_row(idx_ref, s, r):
        return pltpu.make_async_copy(x_hbm.at[idx_ref[0, 0, r]], xbuf.at[s, r], gsem.at[s])

    def scatter_row(idx_ref, s, r):
        return pltpu.make_async_copy(ybuf.at[s, r], ys_hbm.at[idx_ref[0, 0, r]], ssem.at[s])

    def rolled(row_copy):
        def body(r, carry):
            row_copy(r).start()
            return carry
        lax.fori_loop(0, TM, body, 0)

    def wait_gather(s):
        pltpu.make_async_copy(x_hbm.at[pl.ds(0, TM)], xbuf.at[s], gsem.at[s]).wait()

    def wait_scatter(s):
        pltpu.make_async_copy(ybuf.at[s], ys_hbm.at[pl.ds(0, TM)], ssem.at[s]).wait()

    @pl.when(i == 0)
    def _():
        rolled(lambda r: gather_row(src_ref, 0, r))
        rolled(lambda r: gather_row(src1_ref, 1, r))
        ybuf[1] = jnp.zeros((TM, *ROW_TILE), _F32)
        fill = pltpu.make_async_copy(ybuf.at[1], ys_hbm.at[pl.ds(n_pairs, TM)], zsem)
        fill.start()
        fill.wait()

    wait_gather(gslot)

    @pl.when(i < n_used)
    def _():
        xn_ref[...] = _rms(xbuf[gslot].reshape(TM, D_MODEL), g_ref[...]).astype(_BF16)

        n_pieces = _swiglu_pieces()

        def neighbour_rows(p):
            for r in range(p * TM // n_pieces, (p + 1) * TM // n_pieces):
                gather_row(src2_ref, gahead, r).start(priority=r % 2)
                scatter_row(dstp_ref, other, r).start(priority=r % 2)

        y = _swiglu(xn_ref, w1_ref.at[0, 0], w3_ref.at[0, 0], w2_ref.at[0, 0], side_work=neighbour_rows)

        @pl.when(i >= 1)
        def _():
            wait_scatter(slot)

        ybuf[slot] = jnp.concatenate(y, axis=1).reshape(TM, *ROW_TILE)

    @pl.when(i >= n_used)
    def _():
        @pl.when(i + GATHER_SLOTS - 1 <= last)
        def _():
            rolled(lambda r: gather_row(src2_ref, gahead, r))

        @pl.when(i == n_used)
        def _():
            rolled(lambda r: scatter_row(dstp_ref, other, r))

    @pl.when(i == last)
    def _():
        for back in range(GATHER_SLOTS - 1):
            @pl.when(last - back < n_used)
            def _():
                wait_gather((last - back + GATHER_SLOTS - 1) % GATHER_SLOTS)

        @pl.when(i < n_used)
        def _():
            rolled(lambda r: scatter_row(dstc_ref, slot, r))

        wait_scatter(0)
        wait_scatter(1)


def _moe(x, g, idx, w1, w3, w2, tile_expert, n_used, src, dst):
    n_tiles = src.shape[0]
    smem_rows = functools.partial(pl.BlockSpec, (1, 1, TM), memory_space=pltpu.SMEM)

    def expert_spec(shape, buffers):
        return pl.BlockSpec((1, 1, *shape), lambda i, te, *_: (idx, te[i], 0, 0),
                            pipeline_mode=pl.Buffered(buffers))

    grid_spec = pltpu.PrefetchScalarGridSpec(
        num_scalar_prefetch=2,
        grid=(n_tiles,),
        in_specs=[
            smem_rows(lambda i, *_: (i, 0, 0)),
            smem_rows(lambda i, *_: (jnp.minimum(i + 1, n_tiles - 1), 0, 0)),
            smem_rows(lambda i, *_: (jnp.minimum(i + GATHER_SLOTS - 1, n_tiles - 1), 0, 0)),
            smem_rows(lambda i, *_: (i, 0, 0)),
            smem_rows(lambda i, *_: (i + 1, 0, 0)),
            pl.BlockSpec(memory_space=pl.ANY),
            pl.BlockSpec((1, D_MODEL), lambda i, *_: (0, 0)),
            expert_spec((D_MODEL, D_FF), 1), expert_spec((D_MODEL, D_FF), 1),
            expert_spec((D_FF, D_MODEL), 1),
        ],
        out_specs=pl.BlockSpec(memory_space=pl.ANY),
        scratch_shapes=[pltpu.VMEM((GATHER_SLOTS, TM, *ROW_TILE), _F32),
                        pltpu.VMEM((2, TM, *ROW_TILE), _F32),
                        pltpu.VMEM((TM, D_MODEL), _BF16),
                        pltpu.SemaphoreType.DMA((GATHER_SLOTS,)),
                        pltpu.SemaphoreType.DMA((2,)),
                        pltpu.SemaphoreType.DMA(())],
    )
    return pl.pallas_call(
        _moe_kernel,
        grid_spec=grid_spec,
        out_shape=jax.ShapeDtypeStruct((TOP_K * x.shape[0] + 2 * TM, *ROW_TILE), _F32),
        compiler_params=pltpu.CompilerParams(
            dimension_semantics=("arbitrary",), vmem_limit_bytes=VMEM_LIMIT),
        name="moe",
    )(tile_expert, n_used, src, src, src, dst, dst, x, g, w1, w3, w2)


def _final_kernel(*refs):
    *in_refs, g_ref, o_ref = refs
    o_ref[...] = _rms(_residual_in(in_refs), g_ref[...])


def _final_norm(h, moe_out, g):
    n_rows = h.shape[0]
    arrays, descr = _layer_inputs(h, moe_out, n_rows)
    return pl.pallas_call(
        _final_kernel,
        grid=(n_rows // TM,),
        in_specs=_tile_specs(descr, n_rows, TM)[:len(descr)] + [_const_spec((1, D_MODEL))],
        out_specs=pl.BlockSpec((TM, D_MODEL), lambda i: (i, 0)),
        out_shape=jax.ShapeDtypeStruct((n_rows, D_MODEL), _F32),
        compiler_params=pltpu.CompilerParams(
            dimension_semantics=("arbitrary",), vmem_limit_bytes=VMEM_LIMIT),
        name="final_norm",
    )(*arrays, g)


def _row(v):
    return v.reshape(1, -1)


def kernel(x, norm_mix_g, norm_ffn_g, ev_w_in, ev_ws, ev_bs, ev_ln_a_g, ev_ln_a_b, ev_conv_w,
           ev_conv_b, ev_ln_b_g, ev_ln_b_b, ev_w_out, od_w_in, od_conv_w, od_w_out, ffn_w1,
           ffn_w3, ffn_w2, moe_router, moe_w1, moe_w3, moe_w2, final_g):
    bsz, seq, d = x.shape
    n_rows = bsz * seq
    depth = norm_mix_g.shape[0]
    assert d == D_MODEL and seq % TM == 0 and seq % TM_C == 0 and depth % 2 == 0 and TOP_K == 2

    h = x.reshape(n_rows, d)
    moe_out = None
    for layer in range(depth):
        i = layer // 2
        if layer % 2 == 0:
            p = dict(
                g=_row(norm_mix_g[layer]), w_in=ev_w_in, ws=ev_ws,
                bsb=jnp.broadcast_to(ev_bs[i][:, :, None], (A_HEADS, CHUNK, LANES)),
                ln_a_g=_row(ev_ln_a_g[i]), ln_a_b=_row(ev_ln_a_b[i]), conv_w=ev_conv_w[i],
                conv_b=_row(ev_conv_b[i]), ln_b_g=_row(ev_ln_b_g[i]), ln_b_b=_row(ev_ln_b_b[i]),
                w_out=ev_w_out)
            h = _mixer_ab(h, moe_out, n_rows, seq, i, p)
            h = _ffn(h, _row(norm_ffn_g[layer]), i, ffn_w1, ffn_w3, ffn_w2)
            moe_out = None
        else:
            router = jnp.pad(moe_router[i], ((0, 0), (0, LANES - N_EXPERTS)))
            r_hi = router.astype(_BF16)
            r_lo = (router - r_hi.astype(_F32)).astype(_BF16)
            p = dict(g=_row(norm_mix_g[layer]), w_in=od_w_in, conv_w=od_conv_w[i], w_out=od_w_out,
                     g_ffn=_row(norm_ffn_g[layer]), r_cat=jnp.concatenate([r_hi, r_lo], axis=1))
            h, route = _mixer_c(h, seq, i, p)
            ys = _moe(h, _row(norm_ffn_g[layer]), i, moe_w1, moe_w3, moe_w2, *_route_plan(route, n_rows))
            moe_out = (ys, route)
    out = _final_norm(h, moe_out, _row(final_g))
    return out.reshape(bsz, seq, d)
```

```python
import functools

import jax
import jax.numpy as jnp
from jax import lax
from jax.experimental import pallas as pl
from jax.experimental.pallas import tpu as pltpu

D_MODEL = 1024
A_WIDTH = 512
A_HEADS = 4
CHUNK = 128
B_WIDTH = 512
CONV_B_WIDTH = 31
CONV_B_PAD = 15
CONV_C_WIDTH = 3
D_FF = 2816
N_EXPERTS = 8
TOP_K = 2
RMS_EPS = 1e-6
LN_EPS = 1e-5

LANES = 128
SUBLANES = 8
MXU_N = 256
D_BLOCKS = D_MODEL // LANES
assert D_BLOCKS == SUBLANES
TM = 512
TM_C = 1024
TM_F = 1024
HALO_B = 16
HALO_C = 8
FF_SPLITS = (0, 512, 1024, 1536, 2048, 2560, D_FF)
CONV_ROWS = 256
GATHER_SLOTS = 3
MIX_ROWS = 512
VMEM_LIMIT = 60 * 1024 * 1024
GATE_LANE = TOP_K

_F32 = jnp.float32
_BF16 = jnp.bfloat16


def _rms(x, g):
    y = x * lax.rsqrt(jnp.mean(x * x, axis=-1, keepdims=True) + RMS_EPS)
    return y * g


def _layernorm(x, g, b):
    mu = jnp.mean(x, axis=-1, keepdims=True)
    xc = x - mu
    y = xc * lax.rsqrt(jnp.mean(xc * xc, axis=-1, keepdims=True) + LN_EPS)
    return y * g + b


def _sigmoid(x):
    return 1.0 / (1.0 + jnp.exp(-x))


def _gelu_tanh(x):
    c = 0.7978845608028654
    return 0.5 * x * (1.0 + jnp.tanh(c * (x + 0.044715 * (x * x * x))))


def _dot(a, b):
    return jnp.dot(a, b, preferred_element_type=_F32)


def _residual_in(refs, rows=slice(None)):
    if len(refs) == 1:
        return refs[0][rows, :]
    x, ya, yb = (ref[rows].reshape(-1, D_MODEL) for ref in refs[:3])
    r = refs[3][rows, :]
    return x + r[:, GATE_LANE:GATE_LANE + 1] * ya + r[:, GATE_LANE + 1:GATE_LANE + 2] * yb


def _mixer_ab_kernel(n_in, tiles_per_seq, *refs):
    mains = refs[0:n_in]
    prevs = refs[n_in:2 * n_in]
    nexts = refs[2 * n_in:3 * n_in]
    (g_ref, w_in_f32, ws_f32, bsb_ref, lnag_ref, lnab_ref, cw_ref, cb_ref,
     lnbg_ref, lnbb_ref, w_out_f32, o_ref, glu_scr, shift_scr, mix_scr,
     w_in_ref, ws_ref, w_out_ref) = refs[3 * n_in:]

    i = pl.program_id(0)
    first = (i % tiles_per_seq) == 0
    last = (i % tiles_per_seq) == tiles_per_seq - 1

    @pl.when(i == 0)
    def _():
        w_in_ref[...] = w_in_f32[0].astype(_BF16)
        ws_ref[...] = ws_f32[0].astype(_BF16)
        w_out_ref[...] = w_out_f32[0].astype(_BF16)

    x = _residual_in(mains)
    g = g_ref[...]
    xn = _rms(x, g).astype(_BF16)
    z = _dot(xn, w_in_ref[...])

    xh = jnp.concatenate([_residual_in(prevs), _residual_in(nexts)], axis=0)
    xhn = _rms(xh, g).astype(_BF16)
    zh = _dot(xhn, w_in_ref[:, 2 * A_WIDTH:])
    glu_h = zh[:, :B_WIDTH] * _sigmoid(zh[:, B_WIDTH:])
    glu_scr[0:HALO_B, :] = jnp.where(first, 0.0, glu_h[:HALO_B])
    glu_scr[HALO_B + TM:, :] = jnp.where(last, 0.0, glu_h[HALO_B:])
    glu_scr[HALO_B:HALO_B + TM, :] = (
        z[:, 2 * A_WIDTH:2 * A_WIDTH + B_WIDTH] * _sigmoid(z[:, 2 * A_WIDTH + B_WIDTH:]))

    a_u = _gelu_tanh(z[:, :A_WIDTH])
    a_v = _layernorm(_gelu_tanh(z[:, A_WIDTH:2 * A_WIDTH]), lnag_ref[...], lnab_ref[...])
    a_vb = a_v.astype(_BF16)
    for c in range(TM // CHUNK):
        rows = slice(c * CHUNK, (c + 1) * CHUNK)
        for h in range(A_HEADS):
            cols = slice(h * LANES, (h + 1) * LANES)
            v = _dot(ws_ref[h], a_vb[rows, cols]) + bsb_ref[h]
            mix_scr[rows, cols] = (a_u[rows, cols] * v).astype(_BF16)

    n_shift_rows = TM + 2 * HALO_B - SUBLANES
    for s in range(1, SUBLANES):
        shift_scr[s - 1] = glu_scr[s:s + n_shift_rows, :]
    cb = cb_ref[...]
    for r in range(TM // CONV_ROWS):
        base = r * CONV_ROWS
        acc = jnp.zeros((CONV_ROWS, B_WIDTH), _F32)
        for k in range(CONV_B_WIDTH):
            off = base + HALO_B - CONV_B_PAD + k
            s = off % SUBLANES
            src = glu_scr if s == 0 else shift_scr.at[s - 1]
            acc = acc + src[off - s:off - s + CONV_ROWS, :] * cw_ref[k:k + 1, :]
        y = _layernorm(acc + cb, lnbg_ref[...], lnbb_ref[...])
        mix_scr[base:base + CONV_ROWS, A_WIDTH:] = (y * _sigmoid(y)).astype(_BF16)

    o_ref[...] = x + _dot(mix_scr[...], w_out_ref[...])


def _const_spec(shape):
    nd = len(shape)
    return pl.BlockSpec(shape, lambda *_: (0,) * nd, pipeline_mode=pl.Buffered(1))


def _layer_spec(shape, idx):
    nd = len(shape)
    return pl.BlockSpec((1, *shape), lambda *_: (idx,) + (0,) * nd, pipeline_mode=pl.Buffered(1))


ROW = (D_MODEL,)
ROW_TILE = (D_BLOCKS, LANES)


def _tile_specs(rows_and_offsets, n_rows, halo, tm=TM):
    nb = n_rows // halo
    per = tm // halo
    mains, prevs, nexts = [], [], []
    for row, off in rows_and_offsets:
        zeros = (0,) * len(row)
        mains.append(pl.BlockSpec((tm, *row), lambda i, o=off, z=zeros: (i + o // tm, *z)))
        prevs.append(pl.BlockSpec(
            (halo, *row), lambda i, o=off, z=zeros: (jnp.maximum(i * per - 1, 0) + o // halo, *z)))
        nexts.append(pl.BlockSpec(
            (halo, *row),
            lambda i, o=off, z=zeros: (jnp.minimum((i + 1) * per, nb - 1) + o // halo, *z)))
    return mains + prevs + nexts


def _layer_inputs(h, moe_out, n_rows):
    if moe_out is None:
        return [h], [(ROW, 0)]
    ys, route = moe_out
    return [h, ys, ys, route], [(ROW_TILE, 0), (ROW_TILE, 0), (ROW_TILE, n_rows), ((LANES,), 0)]


def _mixer_ab(h, moe_out, n_rows, seq, idx, p):
    arrays, descr = _layer_inputs(h, moe_out, n_rows)
    in_specs = _tile_specs(descr, n_rows, HALO_B) + [
        _const_spec((1, D_MODEL)),
        _layer_spec((D_MODEL, 2 * A_WIDTH + 2 * B_WIDTH), idx),
        _layer_spec((A_HEADS, CHUNK, CHUNK), idx),
        _const_spec((A_HEADS, CHUNK, LANES)),
        _const_spec((1, A_WIDTH)), _const_spec((1, A_WIDTH)),
        _const_spec((CONV_B_WIDTH, B_WIDTH)), _const_spec((1, B_WIDTH)),
        _const_spec((1, B_WIDTH)), _const_spec((1, B_WIDTH)),
        _layer_spec((A_WIDTH + B_WIDTH, D_MODEL), idx),
    ]
    return pl.pallas_call(
        functools.partial(_mixer_ab_kernel, len(arrays), seq // TM),
        grid=(n_rows // TM,),
        in_specs=in_specs,
        out_specs=pl.BlockSpec((TM, D_MODEL), lambda i: (i, 0)),
        out_shape=jax.ShapeDtypeStruct((n_rows, D_MODEL), _F32),
        scratch_shapes=[pltpu.VMEM((TM + 2 * HALO_B, B_WIDTH), _F32),
                        pltpu.VMEM((SUBLANES - 1, TM + 2 * HALO_B - SUBLANES, B_WIDTH), _F32),
                        pltpu.VMEM((TM, A_WIDTH + B_WIDTH), _BF16),
                        pltpu.VMEM((D_MODEL, 2 * A_WIDTH + 2 * B_WIDTH), _BF16),
                        pltpu.VMEM((A_HEADS, CHUNK, CHUNK), _BF16),
                        pltpu.VMEM((A_WIDTH + B_WIDTH, D_MODEL), _BF16)],
        compiler_params=pltpu.CompilerParams(
            dimension_semantics=("arbitrary",), vmem_limit_bytes=VMEM_LIMIT),
        name="mixer_ab",
    )(*(arrays * 3), p["g"], p["w_in"], p["ws"], p["bsb"], p["ln_a_g"], p["ln_a_b"],
      p["conv_w"], p["conv_b"], p["ln_b_g"], p["ln_b_b"], p["w_out"])


def _swiglu_pieces():
    n = 0
    for lo, hi in zip(FF_SPLITS[:-1], FF_SPLITS[1:]):
        n += 2 * ((hi - lo) // MXU_N) + D_MODEL // MXU_N
    return n


def _swiglu(xn_ref, w1_ref, w3_ref, w2_ref, side_work=None):
    piece = [0]

    def dot_cols(a, w_ref, rows, lo, hi):
        outs = []
        for c in range(lo, hi, MXU_N):
            if side_work is not None:
                side_work(piece[0])
            piece[0] += 1
            outs.append(_dot(a(), w_ref[rows, c:c + MXU_N].astype(_BF16)))
        return outs

    acc = None
    for lo, hi in zip(FF_SPLITS[:-1], FF_SPLITS[1:]):
        h1 = jnp.concatenate(dot_cols(lambda: xn_ref[...], w1_ref, slice(None), lo, hi), axis=1)
        h3 = jnp.concatenate(dot_cols(lambda: xn_ref[...], w3_ref, slice(None), lo, hi), axis=1)
        a = (h1 * _sigmoid(h1) * h3).astype(_BF16)
        d = dot_cols(lambda: a, w2_ref, slice(lo, hi), 0, D_MODEL)
        acc = d if acc is None else [p + q for p, q in zip(acc, d)]
    return acc


def _ffn_kernel(x_ref, g_ref, w1_ref, w3_ref, w2_ref, o_ref, xn_ref):
    x = x_ref[...]
    xn_ref[...] = _rms(x, g_ref[...]).astype(_BF16)
    y = _swiglu(xn_ref, w1_ref.at[0], w3_ref.at[0], w2_ref.at[0])
    o_ref[...] = x + jnp.concatenate(y, axis=1)


def _ffn(x, g, idx, w1, w3, w2):
    n_rows = x.shape[0]
    return pl.pallas_call(
        _ffn_kernel,
        grid=(n_rows // TM,),
        in_specs=[pl.BlockSpec((TM, D_MODEL), lambda i: (i, 0)),
                  _const_spec((1, D_MODEL)),
                  _layer_spec((D_MODEL, D_FF), idx), _layer_spec((D_MODEL, D_FF), idx),
                  _layer_spec((D_FF, D_MODEL), idx)],
        out_specs=pl.BlockSpec((TM, D_MODEL), lambda i: (i, 0)),
        out_shape=jax.ShapeDtypeStruct((n_rows, D_MODEL), _F32),
        scratch_shapes=[pltpu.VMEM((TM, D_MODEL), _BF16)],
        compiler_params=pltpu.CompilerParams(
            dimension_semantics=("arbitrary",), vmem_limit_bytes=VMEM_LIMIT),
        name="ffn",
    )(x, g, w1, w3, w2)


def _mixer_c_kernel(tiles_per_seq, x_ref, xp_ref, xq_ref, g_ref, w_in_f32, cw_ref, w_out_f32,
                    g2_ref, rcat_ref, o_ref, route_ref, u_scr, w_in_ref, w_out_ref):
    i = pl.program_id(0)
    first = (i % tiles_per_seq) == 0
    last = (i % tiles_per_seq) == tiles_per_seq - 1

    @pl.when(i == 0)
    def _():
        w_in_ref[...] = w_in_f32[0].astype(_BF16)
        w_out_ref[...] = w_out_f32[0].astype(_BF16)

    g = g_ref[...]
    xh = jnp.concatenate([xp_ref[...], xq_ref[...]], axis=0)
    xhn = _rms(xh, g).astype(_BF16)
    zh = _dot(xhn, w_in_ref[:, D_MODEL:])
    uh = zh[:, :D_MODEL] * zh[:, D_MODEL:]
    u_scr[0:HALO_C, :] = jnp.where(first, 0.0, uh[:HALO_C])
    u_scr[HALO_C + TM_C:, :] = jnp.where(last, 0.0, uh[HALO_C:])

    blocks = [slice(b * MIX_ROWS, (b + 1) * MIX_ROWS) for b in range(TM_C // MIX_ROWS)]
    gate_b = []
    for rows in blocks:
        xn = _rms(x_ref[rows, :], g).astype(_BF16)
        z = _dot(xn, w_in_ref[...])
        u_scr[HALO_C + rows.start:HALO_C + rows.stop, :] = z[:, D_MODEL:2 * D_MODEL] * z[:, 2 * D_MODEL:]
        gate_b.append(z[:, :D_MODEL])

    for rows, gb in zip(blocks, gate_b):
        lo_row = HALO_C + rows.start
        conv = (u_scr[lo_row - 1:lo_row - 1 + MIX_ROWS, :] * cw_ref[0:1, :]
                + u_scr[lo_row:lo_row + MIX_ROWS, :] * cw_ref[1:2, :]
                + u_scr[lo_row + 1:lo_row + 1 + MIX_ROWS, :] * cw_ref[2:3, :])
        y = (gb * conv).astype(_BF16)
        x1 = x_ref[rows, :] + _dot(y, w_out_ref[...])
        o_ref[rows] = x1.reshape(MIX_ROWS, *ROW_TILE)

        hn = _rms(x1, g2_ref[...])
        hi = hn.astype(_BF16)
        lo = (hn - hi.astype(_F32)).astype(_BF16)
        by_hi = _dot(hi, rcat_ref[...])
        logits = by_hi[:, :LANES] + (_dot(lo, rcat_ref[:, :LANES]) + by_hi[:, LANES:])

        lane = lax.broadcasted_iota(jnp.int32, logits.shape, 1).astype(_F32)
        neg = jnp.float32(-jnp.inf)
        lg = jnp.where(lane < N_EXPERTS, logits, neg)
        m1 = jnp.max(lg, axis=-1, keepdims=True)
        i1 = jnp.min(jnp.where(lg == m1, lane, float(LANES)), axis=-1, keepdims=True)
        lg2 = jnp.where(lane == i1, neg, lg)
        m2 = jnp.max(lg2, axis=-1, keepdims=True)
        i2 = jnp.min(jnp.where(lg2 == m2, lane, float(LANES)), axis=-1, keepdims=True)
        e2 = jnp.exp(m2 - m1)
        den = 1.0 + e2
        route_ref[rows, :] = jnp.where(lane == 0, i1, jnp.where(lane == 1, i2, jnp.where(
            lane == GATE_LANE, 1.0 / den, jnp.where(lane == GATE_LANE + 1, e2 / den, 0.0))))


def _mixer_c(x, seq, idx, p):
    n_rows = x.shape[0]
    in_specs = _tile_specs([(ROW, 0)], n_rows, HALO_C, TM_C) + [
        _const_spec((1, D_MODEL)),
        _layer_spec((D_MODEL, 3 * D_MODEL), idx),
        _const_spec((CONV_C_WIDTH, D_MODEL)),
        _layer_spec((D_MODEL, D_MODEL), idx),
        _const_spec((1, D_MODEL)),
        _const_spec((D_MODEL, 2 * LANES)),
    ]
    return pl.pallas_call(
        functools.partial(_mixer_c_kernel, seq // TM_C),
        grid=(n_rows // TM_C,),
        in_specs=in_specs,
        out_specs=[pl.BlockSpec((TM_C, *ROW_TILE), lambda i: (i, 0, 0)),
                   pl.BlockSpec((TM_C, LANES), lambda i: (i, 0))],
        out_shape=[jax.ShapeDtypeStruct((n_rows, *ROW_TILE), _F32),
                   jax.ShapeDtypeStruct((n_rows, LANES), _F32)],
        scratch_shapes=[pltpu.VMEM((TM_C + 2 * HALO_C, D_MODEL), _F32),
                        pltpu.VMEM((D_MODEL, 3 * D_MODEL), _BF16),
                        pltpu.VMEM((D_MODEL, D_MODEL), _BF16)],
        compiler_params=pltpu.CompilerParams(
            dimension_semantics=("arbitrary",), vmem_limit_bytes=VMEM_LIMIT),
        name="mixer_c",
    )(x, x, x, p["g"], p["w_in"], p["conv_w"], p["w_out"], p["g_ffn"], p["r_cat"])


def _plan_kernel(te_ref, r0_ref, last_ref, cum_ref, u_ref):
    del te_ref
    i = pl.program_id(0)
    n_blocks = last_ref.shape[-1]
    rank = (r0_ref[i] + lax.broadcasted_iota(jnp.int32, (TM, 1), 0)).astype(_F32)
    blk = jnp.sum(jnp.where(last_ref[0] <= rank, 1.0, 0.0), axis=1, keepdims=True)
    block_ids = lax.broadcasted_iota(jnp.int32, (TM, n_blocks), 1).astype(_F32)
    onehot = jnp.where(block_ids == blk, 1.0, 0.0).astype(_BF16)
    digits = _dot(onehot, cum_ref[0])
    cum = digits[:, :LANES] * 256.0 + digits[:, LANES:]
    within = jnp.sum(jnp.where(cum <= rank, 1.0, 0.0), axis=1, keepdims=True)
    u = jnp.minimum(blk * LANES + within, n_blocks * LANES - 1.0)
    u_ref[0] = jnp.broadcast_to(u, (TM, LANES)).T[0:1, :].astype(jnp.int32)


def _route_plan(route, n_rows):
    n_pairs = TOP_K * n_rows
    n_blocks = n_pairs // LANES
    n_tiles = (n_pairs + N_EXPERTS * (TM - 1)) // TM
    e_cat = jnp.concatenate([route[:, k] for k in range(TOP_K)]).astype(jnp.int32)
    onehot = (e_cat[None, :] == jnp.arange(N_EXPERTS, dtype=jnp.int32)[:, None]).astype(jnp.int32)
    cum = jnp.cumsum(onehot, axis=1)
    counts = cum[:, -1]
    tiles_e = (counts + TM - 1) // TM
    tile_end = jnp.cumsum(tiles_e)
    n_used = tile_end[-1]
    tile_ids = jnp.arange(n_tiles, dtype=jnp.int32)
    tile_expert = jnp.sum(
        (tile_end[None, :] <= jnp.minimum(tile_ids, n_used - 1)[:, None]).astype(jnp.int32), axis=1)
    r0 = (tile_ids - (tile_end - tiles_e)[tile_expert]) * TM
    n_valid = jnp.where(tile_ids < n_used, jnp.clip(counts[tile_expert] - r0, 0, TM), 0)
    cum_blocks = cum.reshape(N_EXPERTS, n_blocks, LANES)
    block_last = cum_blocks[:, :, -1].astype(_F32).reshape(N_EXPERTS, 1, n_blocks)
    cum_digits = jnp.concatenate([cum_blocks // 256, cum_blocks % 256], axis=-1).astype(_BF16)

    u = pl.pallas_call(
        _plan_kernel,
        grid_spec=pltpu.PrefetchScalarGridSpec(
            num_scalar_prefetch=2,
            grid=(n_tiles,),
            in_specs=[pl.BlockSpec((1, 1, n_blocks), lambda i, te, *_: (te[i], 0, 0)),
                      pl.BlockSpec((1, n_blocks, 2 * LANES), lambda i, te, *_: (te[i], 0, 0))],
            out_specs=pl.BlockSpec((1, 1, TM), lambda i, *_: (i, 0, 0)),
        ),
        out_shape=jax.ShapeDtypeStruct((n_tiles, 1, TM), jnp.int32),
        compiler_params=pltpu.CompilerParams(
            dimension_semantics=("arbitrary",), vmem_limit_bytes=VMEM_LIMIT),
        name="plan",
    )(tile_expert, r0, block_last, cum_digits)
    src = jnp.where(u >= n_rows, u - n_rows, u)
    row = jnp.arange(TM, dtype=jnp.int32)[None, None, :]
    spare = n_pairs + (tile_ids % 2)[:, None, None] * TM + row
    dst = jnp.where(row < n_valid[:, None, None], u, spare)
    dst = jnp.concatenate([n_pairs + TM + row, dst], axis=0)
    return tile_expert, n_used.reshape(1), src, dst


def _moe_kernel(te_ref, nu_ref, src_ref, src1_ref, src2_ref, dstp_ref, dstc_ref, x_hbm, g_ref,
                w1_ref, w3_ref, w2_ref, ys_hbm, xbuf, ybuf, xn_ref, gsem, ssem, zsem):
    del te_ref
    i = pl.program_id(0)
    last = pl.num_programs(0) - 1
    n_used = nu_ref[0]
    slot = i % 2
    other = 1 - slot
    gslot = i % GATHER_SLOTS
    gahead = (i + GATHER_SLOTS - 1) % GATHER_SLOTS
    n_pairs = ys_hbm.shape[0] - 2 * TM

    def gather_row(idx_ref, s, r):
        return pltpu.make_async_copy(x_hbm.at[idx_ref[0, 0, r]], xbuf.at[s, r], gsem.at[s])

    def scatter_row(idx_ref, s, r):
        return pltpu.make_async_copy(ybuf.at[s, r], ys_hbm.at[idx_ref[0, 0, r]], ssem.at[s])

    def rolled(row_copy):
        def body(r, carry):
            row_copy(r).start()
            return carry
        lax.fori_loop(0, TM, body, 0)

    def wait_gather(s):
        pltpu.make_async_copy(x_hbm.at[pl.ds(0, TM)], xbuf.at[s], gsem.at[s]).wait()

    def wait_scatter(s):
        pltpu.make_async_copy(ybuf.at[s], ys_hbm.at[pl.ds(0, TM)], ssem.at[s]).wait()

    @pl.when(i == 0)
    def _():
        rolled(lambda r: gather_row(src_ref, 0, r))
        rolled(lambda r: gather_row(src1_ref, 1, r))
        ybuf[1] = jnp.zeros((TM, *ROW_TILE), _F32)
        fill = pltpu.make_async_copy(ybuf.at[1], ys_hbm.at[pl.ds(n_pairs, TM)], zsem)
        fill.start()
        fill.wait()

    wait_gather(gslot)

    @pl.when(i < n_used)
    def _():
        xn_ref[...] = _rms(xbuf[gslot].reshape(TM, D_MODEL), g_ref[...]).astype(_BF16)

        n_pieces = _swiglu_pieces()

        def neighbour_rows(p):
            for r in range(p * TM // n_pieces, (p + 1) * TM // n_pieces):
                gather_row(src2_ref, gahead, r).start(priority=r % 2)
                scatter_row(dstp_ref, other, r).start(priority=r % 2)

        y = _swiglu(xn_ref, w1_ref.at[0, 0], w3_ref.at[0, 0], w2_ref.at[0, 0], side_work=neighbour_rows)

        @pl.when(i >= 1)
        def _():
            wait_scatter(slot)

        ybuf[slot] = jnp.concatenate(y, axis=1).reshape(TM, *ROW_TILE)

    @pl.when(i >= n_used)
    def _():
        @pl.when(i + GATHER_SLOTS - 1 <= last)
        def _():
            rolled(lambda r: gather_row(src2_ref, gahead, r))

        @pl.when(i == n_used)
        def _():
            rolled(lambda r: scatter_row(dstp_ref, other, r))

    @pl.when(i == last)
    def _():
        for back in range(GATHER_SLOTS - 1):
            @pl.when(last - back < n_used)
            def _():
                wait_gather((last - back + GATHER_SLOTS - 1) % GATHER_SLOTS)

        @pl.when(i < n_used)
        def _():
            rolled(lambda r: scatter_row(dstc_ref, slot, r))

        wait_scatter(0)
        wait_scatter(1)


def _moe(x, g, idx, w1, w3, w2, tile_expert, n_used, src, dst):
    n_tiles = src.shape[0]
    smem_rows = functools.partial(pl.BlockSpec, (1, 1, TM), memory_space=pltpu.SMEM)

    def expert_spec(shape, buffers):
        return pl.BlockSpec((1, 1, *shape), lambda i, te, *_: (idx, te[i], 0, 0),
                            pipeline_mode=pl.Buffered(buffers))

    grid_spec = pltpu.PrefetchScalarGridSpec(
        num_scalar_prefetch=2,
        grid=(n_tiles,),
        in_specs=[
            smem_rows(lambda i, *_: (i, 0, 0)),
            smem_rows(lambda i, *_: (jnp.minimum(i + 1, n_tiles - 1), 0, 0)),
            smem_rows(lambda i, *_: (jnp.minimum(i + GATHER_SLOTS - 1, n_tiles - 1), 0, 0)),
            smem_rows(lambda i, *_: (i, 0, 0)),
            smem_rows(lambda i, *_: (i + 1, 0, 0)),
            pl.BlockSpec(memory_space=pl.ANY),
            pl.BlockSpec((1, D_MODEL), lambda i, *_: (0, 0)),
            expert_spec((D_MODEL, D_FF), 1), expert_spec((D_MODEL, D_FF), 1),
            expert_spec((D_FF, D_MODEL), 1),
        ],
        out_specs=pl.BlockSpec(memory_space=pl.ANY),
        scratch_shapes=[pltpu.VMEM((GATHER_SLOTS, TM, *ROW_TILE), _F32),
                        pltpu.VMEM((2, TM, *ROW_TILE), _F32),
                        pltpu.VMEM((TM, D_MODEL), _BF16),
                        pltpu.SemaphoreType.DMA((GATHER_SLOTS,)),
                        pltpu.SemaphoreType.DMA((2,)),
                        pltpu.SemaphoreType.DMA(())],
    )
    return pl.pallas_call(
        _moe_kernel,
        grid_spec=grid_spec,
        out_shape=jax.ShapeDtypeStruct((TOP_K * x.shape[0] + 2 * TM, *ROW_TILE), _F32),
        compiler_params=pltpu.CompilerParams(
            dimension_semantics=("arbitrary",), vmem_limit_bytes=VMEM_LIMIT),
        name="moe",
    )(tile_expert, n_used, src, src, src, dst, dst, x, g, w1, w3, w2)


def _final_kernel(*refs):
    *in_refs, g_ref, o_ref = refs
    o_ref[...] = _rms(_residual_in(in_refs), g_ref[...])


def _final_norm(h, moe_out, g):
    n_rows = h.shape[0]
    arrays, descr = _layer_inputs(h, moe_out, n_rows)
    return pl.pallas_call(
        _final_kernel,
        grid=(n_rows // TM_F,),
        in_specs=_tile_specs(descr, n_rows, TM_F, TM_F)[:len(descr)] + [_const_spec((1, D_MODEL))],
        out_specs=pl.BlockSpec((TM_F, D_MODEL), lambda i: (i, 0)),
        out_shape=jax.ShapeDtypeStruct((n_rows, D_MODEL), _F32),
        compiler_params=pltpu.CompilerParams(
            dimension_semantics=("arbitrary",), vmem_limit_bytes=VMEM_LIMIT),
        name="final_norm",
    )(*arrays, g)


def _row(v):
    return v.reshape(1, -1)


def kernel(x, norm_mix_g, norm_ffn_g, ev_w_in, ev_ws, ev_bs, ev_ln_a_g, ev_ln_a_b, ev_conv_w,
           ev_conv_b, ev_ln_b_g, ev_ln_b_b, ev_w_out, od_w_in, od_conv_w, od_w_out, ffn_w1,
           ffn_w3, ffn_w2, moe_router, moe_w1, moe_w3, moe_w2, final_g):
    bsz, seq, d = x.shape
    n_rows = bsz * seq
    depth = norm_mix_g.shape[0]
    assert d == D_MODEL and seq % TM == 0 and seq % TM_C == 0 and depth % 2 == 0 and TOP_K == 2
    assert n_rows % TM_F == 0

    h = x.reshape(n_rows, d)
    moe_out = None
    for layer in range(depth):
        i = layer // 2
        if layer % 2 == 0:
            p = dict(
                g=_row(norm_mix_g[layer]), w_in=ev_w_in, ws=ev_ws,
                bsb=jnp.broadcast_to(ev_bs[i][:, :, None], (A_HEADS, CHUNK, LANES)),
                ln_a_g=_row(ev_ln_a_g[i]), ln_a_b=_row(ev_ln_a_b[i]), conv_w=ev_conv_w[i],
                conv_b=_row(ev_conv_b[i]), ln_b_g=_row(ev_ln_b_g[i]), ln_b_b=_row(ev_ln_b_b[i]),
                w_out=ev_w_out)
            h = _mixer_ab(h, moe_out, n_rows, seq, i, p)
            h = _ffn(h, _row(norm_ffn_g[layer]), i, ffn_w1, ffn_w3, ffn_w2)
            moe_out = None
        else:
            router = jnp.pad(moe_router[i], ((0, 0), (0, LANES - N_EXPERTS)))
            r_hi = router.astype(_BF16)
            r_lo = (router - r_hi.astype(_F32)).astype(_BF16)
            p = dict(g=_row(norm_mix_g[layer]), w_in=od_w_in, conv_w=od_conv_w[i], w_out=od_w_out,
                     g_ffn=_row(norm_ffn_g[layer]), r_cat=jnp.concatenate([r_hi, r_lo], axis=1))
            h, route = _mixer_c(h, seq, i, p)
            ys = _moe(h, _row(norm_ffn_g[layer]), i, moe_w1, moe_w3, moe_w2, *_route_plan(route, n_rows))
            moe_out = (ys, route)
    out = _final_norm(h, moe_out, _row(final_g))
    return out.reshape(bsz, seq, d)
```
